```python
import jax
import jax.numpy as jnp
from jax import lax
import numpy as np

D_MODEL = 1024
BATCH = 8
SEQ = 4096
DEPTH = 2

GRID_W = 64
CTX_LEN = 256
EPS = 1e-6

NA_HEADS = 8
NA_HEAD_DIM = 64
NA_WIN_R = 8
NA_WIN_C = 16
NA_WIDTH = NA_HEADS * NA_HEAD_DIM

CONV_CH = 256
CONV_WIDTH = 31

GLA_HEADS = 4
GLA_DK = 32
GLA_DV = 64
GLA_GATE_RANK = 16
GLA_TAU = 16.0
GLA_CHUNK = 64
GLA_K = GLA_HEADS * GLA_DK
GLA_V = GLA_HEADS * GLA_DV
ROPE_THETA = 10000.0

MIX_WIDTH = NA_WIDTH + CONV_CH + GLA_V
D_FF = 4 * D_MODEL
IN_SPLITS = (NA_WIDTH, NA_WIDTH, NA_WIDTH, 2 * CONV_CH, GLA_K, GLA_K, GLA_V, GLA_V, GLA_GATE_RANK, GLA_GATE_RANK)
IN_WIDTH = 3 * NA_WIDTH + 2 * CONV_CH + 2 * GLA_K + 2 * GLA_V + 2 * GLA_GATE_RANK

kernel_name = 'hymba_style_natten_conformer_gla_dit'


def rms_norm(x, g):
    xf = x.astype(jnp.float32)
    y = xf * lax.rsqrt(jnp.mean(xf * xf, axis=-1, keepdims=True) + EPS)
    return (y * g.astype(jnp.float32)).astype(x.dtype)


def layer_norm(x, g, b):
    xf = x.astype(jnp.float32)
    mu = jnp.mean(xf, axis=-1, keepdims=True)
    var = jnp.mean(jnp.square(xf - mu), axis=-1, keepdims=True)
    y = (xf - mu) * lax.rsqrt(var + EPS)
    return (y * g.astype(jnp.float32) + b.astype(jnp.float32)).astype(x.dtype)


def modulate(h, shift, scale):
    return h * (1.0 + scale) + shift


def split_heads(t, n_heads, head_dim):
    return t.reshape(t.shape[0], t.shape[1], n_heads, head_dim)


def split_columns(u):
    offsets = np.cumsum(IN_SPLITS)[:-1].tolist()
    return jnp.split(u, offsets, axis=-1)


def axial_rope(t):
    n, d = t.shape[1], t.shape[-1]
    half = d // 2
    n_freq = half // 2
    pos = jnp.arange(n)
    inv_freq = ROPE_THETA ** (-jnp.arange(n_freq, dtype=jnp.float32) / n_freq)
    tf = t.astype(jnp.float32)

    def rotate(u, p):
        ang = p.astype(jnp.float32)[:, None] * inv_freq[None, :]
        cos = jnp.cos(ang)[None, :, None, :]
        sin = jnp.sin(ang)[None, :, None, :]
        u1, u2 = u[..., :n_freq], u[..., n_freq:]
        return jnp.concatenate([u1 * cos - u2 * sin, u1 * sin + u2 * cos], axis=-1)

    out = jnp.concatenate([rotate(tf[..., :half], pos // GRID_W), rotate(tf[..., half:], pos % GRID_W)], axis=-1)
    return out.astype(t.dtype)


def neighborhood_attention(q, k, v, k_ctx, v_ctx, rpb, rows):
    b = q.shape[0]
    wr = min(NA_WIN_R, rows)
    n_loc = wr * NA_WIN_C
    qg = q.reshape(b, rows, GRID_W, NA_HEADS, NA_HEAD_DIM)
    kg = k.reshape(b, rows, GRID_W, NA_HEADS, NA_HEAD_DIM)
    vg = v.reshape(b, rows, GRID_W, NA_HEADS, NA_HEAD_DIM)
    cols = jnp.arange(GRID_W)
    col_start = jnp.clip(cols - NA_WIN_C // 2, 0, GRID_W - NA_WIN_C)
    col_idx = col_start[:, None] + jnp.arange(NA_WIN_C)[None, :]
    col_off = col_idx - cols[:, None] + (NA_WIN_C - 1)

    def row_block(r):
        r0 = jnp.clip(r - wr // 2, 0, rows - wr)
        q_r = lax.dynamic_index_in_dim(qg, r, axis=1, keepdims=False)
        k_win = lax.dynamic_slice_in_dim(kg, r0, wr, axis=1)[:, :, col_idx]
        v_win = lax.dynamic_slice_in_dim(vg, r0, wr, axis=1)[:, :, col_idx]
        row_off = r0 + jnp.arange(wr) - r + (NA_WIN_R - 1)
        bias = rpb[:, row_off[:, None, None], col_off[None, :, :]]
        bias = jnp.transpose(bias, (0, 2, 1, 3))[None].astype(jnp.float32)
        s_loc = jnp.einsum('bqhd,brqchd->bhqrc', q_r, k_win).astype(jnp.float32) + bias
        s_ctx = jnp.einsum('bqhd,blhd->bhql', q_r, k_ctx).astype(jnp.float32)
        s = jnp.concatenate([s_loc.reshape(b, NA_HEADS, GRID_W, n_loc), s_ctx], axis=-1)
        p = jax.nn.softmax(s, axis=-1).astype(v.dtype)
        p_loc = p[..., :n_loc].reshape(b, NA_HEADS, GRID_W, wr, NA_WIN_C)
        return (jnp.einsum('bhqrc,brqchd->bqhd', p_loc, v_win)
                + jnp.einsum('bhql,blhd->bqhd', p[..., n_loc:], v_ctx))

    o = lax.map(row_block, jnp.arange(rows))
    return jnp.transpose(o, (1, 0, 2, 3, 4)).reshape(b, rows * GRID_W, NA_WIDTH)


def context_attention(q, k, v):
    s = jnp.einsum('blhd,bmhd->bhlm', q, k).astype(jnp.float32)
    p = jax.nn.softmax(s, axis=-1).astype(v.dtype)
    o = jnp.einsum('bhlm,bmhd->blhd', p, v)
    return o.reshape(o.shape[0], o.shape[1], NA_WIDTH)


def conv_module(u, conv_w, conv_b, ln_g, ln_b, pw_w, pw_b):
    a, g = jnp.split(u, 2, axis=-1)
    h = a * jax.nn.sigmoid(g)
    h = lax.conv_general_dilated(
        h, conv_w[:, None, :].astype(h.dtype), window_strides=(1,),
        padding=[(CONV_WIDTH // 2, CONV_WIDTH // 2)],
        dimension_numbers=('NWC', 'WIO', 'NWC'), feature_group_count=CONV_CH) + conv_b
    h = jax.nn.silu(layer_norm(h, ln_g, ln_b))
    return h @ pw_w + pw_b


def gla_log_decay(z, w, b):
    a = (z @ w + b).astype(jnp.float32)
    return (jax.nn.log_sigmoid(a) / GLA_TAU).reshape(z.shape[0], z.shape[1], GLA_HEADS, GLA_DK)


def gla_chunked_scan(q, k, v, log_a, s0):
    b, n = q.shape[0], q.shape[1]
    n_chunks = n // GLA_CHUNK

    def to_chunks(t):
        t = t.astype(jnp.float32).reshape(b, n_chunks, GLA_CHUNK, t.shape[2], t.shape[3])
        return jnp.transpose(t, (1, 0, 3, 2, 4))

    mask = jnp.tril(jnp.ones((GLA_CHUNK, GLA_CHUNK), dtype=bool))[:, :, None]

    def step(state, inp):
        q_c, k_c, v_c, la_c = inp
        cum = jnp.cumsum(la_c, axis=2)
        rel = jnp.exp(jnp.where(mask, cum[:, :, :, None, :] - cum[:, :, None, :, :], -jnp.inf))
        att = jnp.einsum('bhtd,bhsd,bhtsd->bhts', q_c, k_c, rel)
        out = (jnp.einsum('bhts,bhsv->bhtv', att, v_c)
               + jnp.einsum('bhtd,bhdv->bhtv', q_c * jnp.exp(cum), state))
        last = cum[:, :, -1:, :]
        state = (jnp.exp(last[:, :, 0, :])[..., None] * state
                 + jnp.einsum('bhsd,bhsv->bhdv', k_c * jnp.exp(last - cum), v_c))
        return state, out

    state, out = lax.scan(step, s0, (to_chunks(q), to_chunks(k), to_chunks(v), to_chunks(log_a)))
    out = jnp.transpose(out, (1, 0, 3, 2, 4)).reshape(b, n, GLA_HEADS, GLA_DV)
    return out.astype(v.dtype), state


def gla_bidirectional(q, k, v, la_fwd, la_bwd, s_fwd0, s_bwd0):
    o_f, s_f = gla_chunked_scan(q, k, v, la_fwd, s_fwd0)
    o_b, s_b = gla_chunked_scan(q[:, ::-1], k[:, ::-1], v[:, ::-1], la_bwd[:, ::-1], s_bwd0)
    return o_f + o_b[:, ::-1], s_f, s_b


def gla_output(o, r, g):
    o = rms_norm(o, g)
    return o.reshape(o.shape[0], o.shape[1], GLA_V).astype(r.dtype) * jax.nn.silu(r)


def squared_relu_mlp(h, w1, w2):
    return jnp.square(jax.nn.relu(h @ w1)) @ w2


def hybrid_layer(x, xc, c, c_ctx, p, rows, need_ctx_out):
    b = x.shape[0]
    mod = jax.nn.silu(c) @ p['w_ada'] + p['b_ada']
    mod_c = jax.nn.silu(c_ctx) @ p['w_ada'] + p['b_ada']
    sh_a, sc_a, g_a, sh_m, sc_m, g_m = jnp.split(mod[:, None, :], 6, axis=-1)
    sh_ac, sc_ac, g_ac, sh_mc, sc_mc, g_mc = jnp.split(mod_c, 6, axis=-1)

    h = modulate(rms_norm(x, p['norm1_g']), sh_a, sc_a)
    hc = modulate(rms_norm(xc, p['norm1_g']), sh_ac, sc_ac)
    qa, ka, va, ub, qg, kg, vg, rg, zf, zb = split_columns(h @ p['w_in'])
    qac, kac, vac, ubc, qgc, kgc, vgc, rgc, zfc, zbc = split_columns(hc @ p['w_in'])

    na_scale = NA_HEAD_DIM ** -0.5
    k_ctx = rms_norm(split_heads(kac, NA_HEADS, NA_HEAD_DIM), p['na_k_g'])
    v_ctx = split_heads(vac, NA_HEADS, NA_HEAD_DIM)
    o_a = neighborhood_attention(
        rms_norm(split_heads(qa, NA_HEADS, NA_HEAD_DIM), p['na_q_g']) * na_scale,
        rms_norm(split_heads(ka, NA_HEADS, NA_HEAD_DIM), p['na_k_g']),
        split_heads(va, NA_HEADS, NA_HEAD_DIM), k_ctx, v_ctx, p['na_rpb'], rows)

    o_b = conv_module(ub, p['conv_w'], p['conv_b'], p['conv_ln_g'], p['conv_ln_b'], p['conv_pw_w'], p['conv_pw_b'])

    gla_scale = GLA_DK ** -0.5
    zero_state = jnp.zeros((b, GLA_HEADS, GLA_DK, GLA_DV), jnp.float32)
    o_gc, s_f, s_b = gla_bidirectional(
        split_heads(qgc, GLA_HEADS, GLA_DK) * gla_scale, split_heads(kgc, GLA_HEADS, GLA_DK),
        split_heads(vgc, GLA_HEADS, GLA_DV),
        gla_log_decay(zfc, p['gla_gw_f'], p['gla_gb_f']), gla_log_decay(zbc, p['gla_gw_b'], p['gla_gb_b']),
        zero_state, zero_state)
    o_g, _, _ = gla_bidirectional(
        axial_rope(split_heads(qg, GLA_HEADS, GLA_DK)) * gla_scale, axial_rope(split_heads(kg, GLA_HEADS, GLA_DK)),
        split_heads(vg, GLA_HEADS, GLA_DV),
        gla_log_decay(zf, p['gla_gw_f'], p['gla_gb_f']), gla_log_decay(zb, p['gla_gw_b'], p['gla_gb_b']),
        s_f, s_b)
    o_c = gla_output(o_g, rg, p['gla_out_g'])

    x = x + g_a * (jnp.concatenate([o_a, o_b, o_c], axis=-1) @ p['w_out'])
    x = x + g_m * squared_relu_mlp(modulate(rms_norm(x, p['norm2_g']), sh_m, sc_m), p['w_mlp_in'], p['w_mlp_out'])
    if not need_ctx_out:
        return x, None

    o_ac = context_attention(rms_norm(split_heads(qac, NA_HEADS, NA_HEAD_DIM), p['na_q_g']) * na_scale, k_ctx, v_ctx)
    o_bc = conv_module(ubc, p['conv_w'], p['conv_b'], p['conv_ln_g'], p['conv_ln_b'], p['conv_pw_w'], p['conv_pw_b'])
    o_cc = gla_output(o_gc, rgc, p['gla_out_g'])
    xc = xc + g_ac * (jnp.concatenate([o_ac, o_bc, o_cc], axis=-1) @ p['w_out'])
    xc = xc + g_mc * squared_relu_mlp(modulate(rms_norm(xc, p['norm2_g']), sh_mc, sc_mc), p['w_mlp_in'], p['w_mlp_out'])
    return x, xc


def setup_inputs(seed: int = 0) -> dict:
    key = jax.random.key(seed)
    ks = jax.random.split(key, 32)
    D = D_MODEL

    def nrm(k, shape, s):
        return jax.random.normal(k, shape, jnp.float32) * s

    return {
        'x': nrm(ks[0], (BATCH, SEQ, D), 1.0),
        'c': nrm(ks[1], (BATCH, D), 1.0),
        'ctx': nrm(ks[2], (BATCH, CTX_LEN, D), 1.0),
        'c_ctx': nrm(ks[3], (D,), 1.0),
        'w_ada': nrm(ks[4], (DEPTH, D, 6 * D), D ** -0.5),
        'b_ada': nrm(ks[5], (DEPTH, 6 * D), 0.02),
        'norm1_g': 1.0 + nrm(ks[6], (DEPTH, D), 0.02),
        'w_in': nrm(ks[7], (DEPTH, D, IN_WIDTH), D ** -0.5),
        'na_q_g': 1.0 + nrm(ks[8], (DEPTH, NA_HEAD_DIM), 0.02),
        'na_k_g': 1.0 + nrm(ks[9], (DEPTH, NA_HEAD_DIM), 0.02),
        'na_rpb': nrm(ks[10], (DEPTH, NA_HEADS, 2 * NA_WIN_R - 1, 2 * NA_WIN_C - 1), 0.1),
        'conv_w': nrm(ks[11], (DEPTH, CONV_WIDTH, CONV_CH), CONV_WIDTH ** -0.5),
        'conv_b': nrm(ks[12], (DEPTH, CONV_CH), 0.02),
        'conv_ln_g': 1.0 + nrm(ks[13], (DEPTH, CONV_CH), 0.02),
        'conv_ln_b': nrm(ks[14], (DEPTH, CONV_CH), 0.02),
        'conv_pw_w': nrm(ks[15], (DEPTH, CONV_CH, CONV_CH), CONV_CH ** -0.5),
        'conv_pw_b': nrm(ks[16], (DEPTH, CONV_CH), 0.02),
        'gla_gw_f': nrm(ks[17], (DEPTH, GLA_GATE_RANK, GLA_K), GLA_GATE_RANK ** -0.5),
        'gla_gb_f': nrm(ks[18], (DEPTH, GLA_K), 0.1),
        'gla_gw_b': nrm(ks[19], (DEPTH, GLA_GATE_RANK, GLA_K), GLA_GATE_RANK ** -0.5),
        'gla_gb_b': nrm(ks[20], (DEPTH, GLA_K), 0.1),
        'gla_out_g': 1.0 + nrm(ks[21], (DEPTH, GLA_DV), 0.02),
        'w_out': nrm(ks[22], (DEPTH, MIX_WIDTH, D), MIX_WIDTH ** -0.5),
        'norm2_g': 1.0 + nrm(ks[23], (DEPTH, D), 0.02),
        'w_mlp_in': nrm(ks[24], (DEPTH, D, D_FF), D ** -0.5),
        'w_mlp_out': nrm(ks[25], (DEPTH, D_FF, D), D_FF ** -0.5),
    }


def reference(x, c, ctx, c_ctx, w_ada, b_ada, norm1_g, w_in, na_q_g, na_k_g, na_rpb,
              conv_w, conv_b, conv_ln_g, conv_ln_b, conv_pw_w, conv_pw_b,
              gla_gw_f, gla_gb_f, gla_gw_b, gla_gb_b, gla_out_g, w_out, norm2_g,
              w_mlp_in, w_mlp_out):
    rows = x.shape[1] // GRID_W
    xc = ctx
    for layer in range(DEPTH):
        p = {
            'w_ada': w_ada[layer], 'b_ada': b_ada[layer], 'norm1_g': norm1_g[layer], 'w_in': w_in[layer],
            'na_q_g': na_q_g[layer], 'na_k_g': na_k_g[layer], 'na_rpb': na_rpb[layer],
            'conv_w': conv_w[layer], 'conv_b': conv_b[layer], 'conv_ln_g': conv_ln_g[layer],
            'conv_ln_b': conv_ln_b[layer], 'conv_pw_w': conv_pw_w[layer], 'conv_pw_b': conv_pw_b[layer],
            'gla_gw_f': gla_gw_f[layer], 'gla_gb_f': gla_gb_f[layer], 'gla_gw_b': gla_gw_b[layer],
            'gla_gb_b': gla_gb_b[layer], 'gla_out_g': gla_out_g[layer], 'w_out': w_out[layer],
            'norm2_g': norm2_g[layer], 'w_mlp_in': w_mlp_in[layer], 'w_mlp_out': w_mlp_out[layer],
        }
        x, xc = hybrid_layer(x, xc, c, c_ctx, p, rows, layer < DEPTH - 1)
    return x
```

```python
import functools

import numpy as np
import jax
import jax.numpy as jnp
from jax import lax
from jax.experimental import pallas as pl
from jax.experimental.pallas import tpu as pltpu

F32 = jnp.float32
BF16 = jnp.bfloat16

GRID_W = 64
EPS = 1e-6

NA_HEADS = 8
NA_HEAD_DIM = 64
NA_WIN_R = 8
NA_WIN_C = 16
NA_WIDTH = NA_HEADS * NA_HEAD_DIM

CONV_CH = 256
CONV_WIDTH = 31

GLA_HEADS = 4
GLA_DK = 32
GLA_DV = 64
GLA_GATE_RANK = 16
GLA_TAU = 16.0
GLA_K = GLA_HEADS * GLA_DK
GLA_V = GLA_HEADS * GLA_DV
ROPE_THETA = 10000.0

_C_QA, _C_KA, _C_VA, _C_UB = 0, NA_WIDTH, 2 * NA_WIDTH, 3 * NA_WIDTH
_C_QG = _C_UB + 2 * CONV_CH
_C_KG = _C_QG + GLA_K
_C_VG = _C_KG + GLA_K
_C_RG = _C_VG + GLA_V
_C_Z = _C_RG + GLA_V
IN_WIDTH = _C_Z + 2 * GLA_GATE_RANK
LANES = 128
IN_WIDTH_PAD = -(-IN_WIDTH // LANES) * LANES

GLA_SUB = 16
NA_TILE_ROWS = 4
NA_KEY_ROWS = NA_TILE_ROWS + NA_WIN_R - 1
MASK_NEG = -1e30
VMEM_LIMIT = 56 * 1024 * 1024


def _params(sem, vmem=VMEM_LIMIT):
    return pltpu.CompilerParams(dimension_semantics=sem, vmem_limit_bytes=vmem)


def _silu(t):
    return t * jax.nn.sigmoid(t)


def _dot(a, b):
    return jnp.dot(a, b, preferred_element_type=F32)


def _dot_nt(a, b):
    return lax.dot_general(a, b, (((1,), (1,)), ((), ())), preferred_element_type=F32)


def _ada_kernel(c_ref, w_ref, b_ref, o_ref):
    s = _silu(c_ref[...]).astype(BF16)
    o_ref[0] = _dot(s, w_ref[0].astype(BF16)) + b_ref[0]


def _ada(cc, w_ada, b_ada):
    depth, d, d6 = w_ada.shape
    rows = cc.shape[0]
    tn = 1536 if d6 % 1536 == 0 else d6
    return pl.pallas_call(
        _ada_kernel,
        grid=(depth, d6 // tn),
        in_specs=[
            pl.BlockSpec((rows, d), lambda l, n: (0, 0)),
            pl.BlockSpec((1, d, tn), lambda l, n: (l, 0, n)),
            pl.BlockSpec((1, 1, tn), lambda l, n: (l, 0, n)),
        ],
        out_specs=pl.BlockSpec((1, rows, tn), lambda l, n: (l, 0, n)),
        out_shape=jax.ShapeDtypeStruct((depth, rows, d6), F32),
        compiler_params=_params(("parallel", "parallel")),
        name="ada",
    )(cc, w_ada, b_ada.reshape(depth, 1, d6))


def _block_scan(t, rowm, reverse):
    n = t.shape[0]
    s = 1
    while s < GLA_SUB:
        if reverse:
            t = t + jnp.where(rowm < GLA_SUB - s, pltpu.roll(t, n - s, 0), 0.0)
        else:
            t = t + jnp.where(rowm >= s, pltpu.roll(t, s, 0), 0.0)
        s *= 2
    return t


def _inproj_kernel(*refs, rope):
    if rope:
        (x_ref, mod_ref, g1_ref, w_ref, ones_ref, gq_ref, gk_ref, gw_ref, gb_ref, cos_ref, sin_ref,
         qa_ref, ka_ref, va_ref, hc_ref, qg_ref, kg_ref, vg_ref, sr_ref, cf_ref, tf_ref, cb_ref, tb_ref) = refs
    else:
        (x_ref, mod_ref, g1_ref, w_ref, ones_ref, gq_ref, gk_ref, gw_ref, gb_ref,
         qa_ref, ka_ref, va_ref, hc_ref, qg_ref, kg_ref, vg_ref, sr_ref, cf_ref, tf_ref, cb_ref, tb_ref) = refs
    x = x_ref[0]
    ms = jnp.mean(x * x, axis=-1, keepdims=True)
    h = x * lax.rsqrt(ms + EPS) * g1_ref[...]
    hb = (h * (1.0 + mod_ref[0, 1:2, :]) + mod_ref[0, 0:1, :]).astype(BF16)

    def seg(lo, hi):
        return _dot(hb, w_ref[:, lo:hi])

    def head_norm(t, g_ref):
        ss = _dot((t * t).astype(BF16), ones_ref[...])
        return (t * lax.rsqrt(ss * (1.0 / NA_HEAD_DIM) + EPS) * g_ref[...]).astype(BF16)

    qa_ref[0] = head_norm(seg(_C_QA, _C_KA), gq_ref)
    ka_ref[0] = head_norm(seg(_C_KA, _C_VA), gk_ref)
    va_ref[0] = seg(_C_VA, _C_UB).astype(BF16)

    ub = seg(_C_UB, _C_QG)
    hc_ref[0] = ub[:, :CONV_CH] * jax.nn.sigmoid(ub[:, CONV_CH:])

    qg = seg(_C_QG, _C_KG)
    kg = seg(_C_KG, _C_VG)
    if rope:
        cos = cos_ref[...]
        sin = sin_ref[...]
        first = (lax.broadcasted_iota(jnp.int32, (1, GLA_K), 1) & 15) < 8

        def rot(t):
            partner = jnp.where(first, pltpu.roll(t, GLA_K - 8, 1), pltpu.roll(t, 8, 1))
            return t * cos + partner * sin

        qg = rot(qg)
        kg = rot(kg)
    qg_ref[0] = qg * (GLA_DK ** -0.5)
    kg_ref[0] = kg
    vg_ref[0] = seg(_C_VG, _C_RG)
    sr_ref[0] = _silu(seg(_C_RG, _C_Z))

    z = seg(_C_Z, IN_WIDTH_PAD).astype(BF16)
    a = _dot(z, gw_ref[...]) + gb_ref[...]
    la = (jnp.minimum(a, 0.0) - jnp.log(1.0 + jnp.exp(-jnp.abs(a)))) * (1.0 / GLA_TAU)
    laf = la[:, :GLA_K]
    lab = la[:, GLA_K:]
    rowm = lax.broadcasted_iota(jnp.int32, laf.shape, 0) & (GLA_SUB - 1)
    pf = _block_scan(laf, rowm, False)
    sf = _block_scan(laf, rowm, True)
    pb = _block_scan(lab, rowm, False)
    sb = _block_scan(lab, rowm, True)
    cf_ref[0] = pf
    tf_ref[0] = pf + sf - laf
    cb_ref[0] = sb
    tb_ref[0] = pb + sb - lab


def _inproj(x, mod, lw, rope_tabs, tm):
    b, n, d = x.shape
    nt = n // tm
    rope = rope_tabs is not None
    mod_map = (lambda bi, ti: (bi, 0, 0)) if mod.shape[0] == b else (lambda bi, ti: (0, 0, 0))
    const = lambda bi, ti: (0, 0)
    tile = lambda bi, ti: (bi, ti, 0)
    in_specs = [
        pl.BlockSpec((1, tm, d), tile),
        pl.BlockSpec((1, 6, d), mod_map),
        pl.BlockSpec((1, d), const),
        pl.BlockSpec((d, IN_WIDTH_PAD), const),
        pl.BlockSpec((NA_WIDTH, NA_WIDTH), const),
        pl.BlockSpec((1, NA_WIDTH), const),
        pl.BlockSpec((1, NA_WIDTH), const),
        pl.BlockSpec((LANES, 2 * GLA_K), const),
        pl.BlockSpec((1, 2 * GLA_K), const),
    ]
    args = [x, mod, lw["g1"], lw["w_in"], lw["ones_na"], lw["gq"], lw["gk"], lw["gw"], lw["gb"]]
    if rope:
        in_specs += [pl.BlockSpec((tm, GLA_K), lambda bi, ti: (ti, 0))] * 2
        args += list(rope_tabs)
    widths = [(NA_WIDTH, BF16)] * 3 + [(CONV_CH, F32), (GLA_K, F32), (GLA_K, F32), (GLA_V, F32), (GLA_V, F32)] + [
        (GLA_K, F32)] * 4
    out_specs = [pl.BlockSpec((1, tm, w), tile) for w, _ in widths]
    out_shape = [jax.ShapeDtypeStruct((b, n, w), dt) for w, dt in widths]
    return pl.pallas_call(
        functools.partial(_inproj_kernel, rope=rope),
        grid=(b, nt),
        in_specs=in_specs,
        out_specs=out_specs,
        out_shape=out_shape,
        compiler_params=_params(("parallel", "parallel")),
        name="inproj_rope" if rope else "inproj",
    )(*args)


def _softmax_pv(s_list, v_list):
    m = s_list[0].max(axis=-1, keepdims=True)
    for s in s_list[1:]:
        m = jnp.maximum(m, s.max(axis=-1, keepdims=True))
    l = 0.0
    o = 0.0
    for s, v in zip(s_list, v_list):
        p = jnp.exp(s - m)
        l = l + p.sum(axis=-1, keepdims=True)
        o = o + _dot(p.astype(BF16), v)
    return o / l


def _na_kernel(q_ref, k_ref, v_ref, kc_ref, vc_ref, bias_ref, o_ref, *, n_rows):
    j = pl.program_id(1)
    w0 = jnp.clip(NA_TILE_ROWS * j - NA_WIN_R // 2, 0, n_rows - NA_KEY_ROWS)
    start = pl.multiple_of(w0 * GRID_W, GRID_W)
    nk = NA_KEY_ROWS * GRID_W
    low = lax.broadcasted_iota(jnp.int32, (1, LANES), 1) < NA_HEAD_DIM
    for p in range(NA_WIDTH // LANES):
        lanes = slice(p * LANES, (p + 1) * LANES)
        qp = q_ref[0, :, lanes]
        kp = k_ref[0, pl.ds(start, nk), lanes]
        vp = v_ref[0, pl.ds(start, nk), lanes]
        kcp = kc_ref[0, :, lanes]
        vcp = vc_ref[0, :, lanes]
        outs = []
        for e in range(2):
            qe = jnp.where(low if e == 0 else jnp.logical_not(low), qp, jnp.zeros_like(qp))
            s_loc = _dot_nt(qe, kp) + bias_ref[0, 2 * p + e]
            s_ctx = _dot_nt(qe, kcp)
            outs.append(_softmax_pv([s_loc, s_ctx], [vp, vcp]))
        o_ref[0, :, lanes] = jnp.where(low, outs[0], outs[1]).astype(BF16)


def _na_bias_table(rpb, n_rows):
    n_tiles = n_rows // NA_TILE_ROWS
    pats = []
    for jt in (0, 1, n_tiles - 1):
        rq0 = jt * NA_TILE_ROWS
        w0 = int(np.clip(rq0 - NA_WIN_R // 2, 0, n_rows - NA_KEY_ROWS))
        r = rq0 + np.arange(NA_TILE_ROWS)[:, None, None, None]
        cq = np.arange(GRID_W)[None, :, None, None]
        rk = w0 + np.arange(NA_KEY_ROWS)[None, None, :, None]
        ck = np.arange(GRID_W)[None, None, None, :]
        r0 = np.clip(r - NA_WIN_R // 2, 0, n_rows - NA_WIN_R)
        cs = np.clip(cq - NA_WIN_C // 2, 0, GRID_W - NA_WIN_C)
        valid = (rk >= r0) & (rk < r0 + NA_WIN_R) & (ck >= cs) & (ck < cs + NA_WIN_C)
        ri = np.clip(rk - r + (NA_WIN_R - 1), 0, 2 * NA_WIN_R - 2) + 0 * (cq + ck)
        ci = np.clip(ck - cq + (NA_WIN_C - 1), 0, 2 * NA_WIN_C - 2) + 0 * (r + rk)
        tq = NA_TILE_ROWS * GRID_W
        nk = NA_KEY_ROWS * GRID_W
        flat = (ri * (2 * NA_WIN_C - 1) + ci).reshape(tq, nk)
        vals = jnp.take(rpb.reshape(NA_HEADS, -1), jnp.asarray(flat.reshape(-1), jnp.int32), axis=1)
        vals = vals.reshape(NA_HEADS, tq, nk)
        pats.append(jnp.where(jnp.asarray(valid.reshape(tq, nk)), vals, MASK_NEG))
    return jnp.stack(pats).astype(F32)


def _na_attn(qa, ka, va, kc, vc, bias_tab, n_rows):
    b, n, _ = qa.shape
    lc = kc.shape[1]
    n_tiles = n_rows // NA_TILE_ROWS
    tq = NA_TILE_ROWS * GRID_W
    nk = NA_KEY_ROWS * GRID_W

    def bias_map(bi, j):
        return (jnp.where(j == 0, 0, jnp.where(j == n_tiles - 1, 2, 1)), 0, 0, 0)

    return pl.pallas_call(
        functools.partial(_na_kernel, n_rows=n_rows),
        grid=(b, n_tiles),
        in_specs=[
            pl.BlockSpec((1, tq, NA_WIDTH), lambda bi, j: (bi, j, 0)),
            pl.BlockSpec((1, n, NA_WIDTH), lambda bi, j: (bi, 0, 0)),
            pl.BlockSpec((1, n, NA_WIDTH), lambda bi, j: (bi, 0, 0)),
            pl.BlockSpec((1, lc, NA_WIDTH), lambda bi, j: (bi, 0, 0)),
            pl.BlockSpec((1, lc, NA_WIDTH), lambda bi, j: (bi, 0, 0)),
            pl.BlockSpec((1, NA_HEADS, tq, nk), bias_map),
        ],
        out_specs=pl.BlockSpec((1, tq, NA_WIDTH), lambda bi, j: (bi, j, 0)),
        out_shape=jax.ShapeDtypeStruct((b, n, NA_WIDTH), BF16),
        compiler_params=_params(("parallel", "arbitrary")),
        name="na_attn",
    )(qa, ka, va, kc, vc, bias_tab)


def _ctx_attn_kernel(q_ref, k_ref, v_ref, o_ref):
    low = lax.broadcasted_iota(jnp.int32, (1, LANES), 1) < NA_HEAD_DIM
    for p in range(NA_WIDTH // LANES):
        lanes = slice(p * LANES, (p + 1) * LANES)
        qp = q_ref[0, :, lanes]
        kp = k_ref[0, :, lanes]
        vp = v_ref[0, :, lanes]
        outs = []
        for e in range(2):
            qe = jnp.where(low if e == 0 else jnp.logical_not(low), qp, jnp.zeros_like(qp))
            outs.append(_softmax_pv([_dot_nt(qe, kp)], [vp]))
        o_ref[0, :, lanes] = jnp.where(low, outs[0], outs[1]).astype(BF16)


def _ctx_attn(q, k, v):
    b, lc, _ = q.shape
    spec = pl.BlockSpec((1, lc, NA_WIDTH), lambda bi: (bi, 0, 0))
    return pl.pallas_call(
        _ctx_attn_kernel,
        grid=(b,),
        in_specs=[spec, spec, spec],
        out_specs=spec,
        out_shape=jax.ShapeDtypeStruct((b, lc, NA_WIDTH), BF16),
        compiler_params=_params(("parallel",)),
        name="ctx_attn",
    )(q, k, v)


CONV_HALO = 16


def _conv_kernel(h_ref, w_ref, cb_ref, lg_ref, lb_ref, pw_ref, pb_ref, o_ref, hp_ref, *, n, tile):
    hp_ref[0:CONV_HALO, :] = jnp.zeros((CONV_HALO, CONV_CH), F32)
    hp_ref[CONV_HALO + n:, :] = jnp.zeros((CONV_HALO, CONV_CH), F32)
    hp_ref[CONV_HALO:CONV_HALO + n, :] = h_ref[0]
    shift = CONV_HALO - CONV_WIDTH // 2

    def body(i, carry):
        t0 = pl.multiple_of(i * tile, tile)
        chunk = hp_ref[pl.ds(t0, tile + 2 * CONV_HALO), :]
        acc = jnp.zeros((tile, CONV_CH), F32)
        for k in range(CONV_WIDTH):
            acc = acc + chunk[k + shift:k + shift + tile, :] * w_ref[k:k + 1, :]
        hcv = acc + cb_ref[...]
        mu = jnp.mean(hcv, axis=-1, keepdims=True)
        ctr = hcv - mu
        var = jnp.mean(ctr * ctr, axis=-1, keepdims=True)
        y = _silu(ctr * lax.rsqrt(var + EPS) * lg_ref[...] + lb_ref[...])
        o_ref[0, pl.ds(t0, tile), :] = (_dot(y.astype(BF16), pw_ref[...]) + pb_ref[...]).astype(BF16)
        return carry

    lax.fori_loop(0, n // tile, body, 0)


def _conv(hc, lw):
    b, n, _ = hc.shape
    tile = 128
    const = lambda bi: (0, 0)
    vec = pl.BlockSpec((1, CONV_CH), const)
    return pl.pallas_call(
        functools.partial(_conv_kernel, n=n, tile=tile),
        grid=(b,),
        in_specs=[
            pl.BlockSpec((1, n, CONV_CH), lambda bi: (bi, 0, 0)),
            pl.BlockSpec((CONV_WIDTH, CONV_CH), const),
            vec, vec, vec,
            pl.BlockSpec((CONV_CH, CONV_CH), const),
            vec,
        ],
        out_specs=pl.BlockSpec((1, n, CONV_CH), lambda bi: (bi, 0, 0)),
        out_shape=jax.ShapeDtypeStruct((b, n, CONV_CH), BF16),
        scratch_shapes=[pltpu.VMEM((n + 2 * CONV_HALO, CONV_CH), F32)],
        compiler_params=_params(("parallel",)),
        name="conv",
    )(hc, lw["conv_w"], lw["conv_b"], lw["conv_ln_g"], lw["conv_ln_b"], lw["conv_pw_w"], lw["conv_pw_b"])


GLA_INTRA_ROWS = 64


def _gla_intra(q_ref, k_ref, v_ref, c_ref, exp_ref, o_ref, r0, rowm, forward):
    sub = GLA_INTRA_ROWS
    q = q_ref[0, pl.ds(r0, sub), :]
    k = k_ref[0, pl.ds(r0, sub), :]
    v = v_ref[0, pl.ds(r0, sub), :]
    cum = c_ref[0, pl.ds(r0, sub), :]
    acc = _dot((q * k).astype(BF16), exp_ref[...]) * v
    for d in range(1, GLA_SUB):
        sh = d if forward else sub - d
        valid = (rowm >= d) if forward else (rowm < GLA_SUB - d)
        e = jnp.exp(jnp.where(valid, cum - pltpu.roll(cum, sh, 0), MASK_NEG))
        pd = q * pltpu.roll(k, sh, 0) * e
        acc = acc + _dot(pd.astype(BF16), exp_ref[...]) * pltpu.roll(v, sh, 0)
    o_ref[0, pl.ds(r0, sub), :] = acc


def _gla_prepare(q_ref, k_ref, v_ref, c_ref, t_ref, qt_ref, kh_ref, vt_ref):
    cum = c_ref[0]
    qt_ref[...] = (q_ref[0] * jnp.exp(cum)).astype(BF16)
    kh_ref[...] = (k_ref[0] * jnp.exp(t_ref[0] - cum)).astype(BF16)
    vt_ref[...] = v_ref[0].T.astype(BF16)


def _gla_step(i, t_ref, o_ref, s_ref, qt_ref, kh_ref, vt_ref, same_head, chunk_of_lane):
    r0 = pl.multiple_of(i * GLA_SUB, GLA_SUB)
    rows = pl.ds(r0, GLA_SUB)
    state = s_ref[...]
    sb = jnp.where(same_head, state, 0.0).astype(BF16)
    o_ref[0, rows, :] = o_ref[0, rows, :] + _dot_nt(qt_ref[rows, :], sb)
    vtm = jnp.where(chunk_of_lane == i, vt_ref[...], jnp.zeros((), BF16))
    s_ref[...] = jnp.exp(t_ref[0, pl.ds(r0, 1), :]) * state + _dot(vtm, kh_ref[...])


def _gla_kernel(qf_ref, kf_ref, vf_ref, cf_ref, tf_ref, qb_ref, kb_ref, vb_ref, cb_ref, tb_ref, s0f_ref, s0b_ref,
                exp_ref, of_ref, ob_ref, sfo_ref, sbo_ref, sf_ref, sb_ref, qtf_ref, khf_ref, vtf_ref, qtb_ref,
                khb_ref, vtb_ref):
    j = pl.program_id(1)
    tn = qf_ref.shape[1]

    @pl.when(j == 0)
    def _():
        sf_ref[...] = s0f_ref[0]
        sb_ref[...] = s0b_ref[0]

    rowm = lax.broadcasted_iota(jnp.int32, (GLA_INTRA_ROWS, GLA_K), 0) & (GLA_SUB - 1)

    def intra(ib, carry):
        r0 = pl.multiple_of(ib * GLA_INTRA_ROWS, GLA_INTRA_ROWS)
        _gla_intra(qf_ref, kf_ref, vf_ref, cf_ref, exp_ref, of_ref, r0, rowm, True)
        _gla_intra(qb_ref, kb_ref, vb_ref, cb_ref, exp_ref, ob_ref, r0, rowm, False)
        return carry

    lax.fori_loop(0, tn // GLA_INTRA_ROWS, intra, 0)

    _gla_prepare(qf_ref, kf_ref, vf_ref, cf_ref, tf_ref, qtf_ref, khf_ref, vtf_ref)
    _gla_prepare(qb_ref, kb_ref, vb_ref, cb_ref, tb_ref, qtb_ref, khb_ref, vtb_ref)
    n_chunks = tn // GLA_SUB
    head_of_row = lax.broadcasted_iota(jnp.int32, (GLA_V, GLA_K), 0) // GLA_DV
    head_of_col = lax.broadcasted_iota(jnp.int32, (GLA_V, GLA_K), 1) // GLA_DK
    same_head = head_of_row == head_of_col
    chunk_of_lane = lax.broadcasted_iota(jnp.int32, (1, tn), 1) // GLA_SUB

    def step(i, carry):
        _gla_step(i, tf_ref, of_ref, sf_ref, qtf_ref, khf_ref, vtf_ref, same_head, chunk_of_lane)
        _gla_step(n_chunks - 1 - i, tb_ref, ob_ref, sb_ref, qtb_ref, khb_ref, vtb_ref, same_head, chunk_of_lane)
        return carry

    lax.fori_loop(0, n_chunks, step, 0)

    @pl.when(j == pl.num_programs(1) - 1)
    def _():
        sfo_ref[0] = sf_ref[...]
        sbo_ref[0] = sb_ref[...]


def _gla(qg, kg, vg, cf, tf, cb, tb, s0f, s0b, expand, tn):
    b, n, _ = qg.shape
    nt = n // tn
    fwd = lambda bi, j: (bi, j, 0)
    bwd = lambda bi, j: (bi, nt - 1 - j, 0)
    per_b = lambda bi, j: (bi, 0, 0)

    def spec(w, m):
        return pl.BlockSpec((1, tn, w), m)

    state_spec = pl.BlockSpec((1, GLA_V, GLA_K), per_b)
    return pl.pallas_call(
        _gla_kernel,
        grid=(b, nt),
        in_specs=[spec(GLA_K, fwd), spec(GLA_K, fwd), spec(GLA_V, fwd), spec(GLA_K, fwd), spec(GLA_K, fwd),
                  spec(GLA_K, bwd), spec(GLA_K, bwd), spec(GLA_V, bwd), spec(GLA_K, bwd), spec(GLA_K, bwd),
                  state_spec, state_spec,
                  pl.BlockSpec((GLA_K, GLA_V), lambda bi, j: (0, 0))],
        out_specs=[spec(GLA_V, fwd), spec(GLA_V, bwd), state_spec, state_spec],
        out_shape=[jax.ShapeDtypeStruct((b, n, GLA_V), F32), jax.ShapeDtypeStruct((b, n, GLA_V), F32),
                   jax.ShapeDtypeStruct((b, GLA_V, GLA_K), F32), jax.ShapeDtypeStruct((b, GLA_V, GLA_K), F32)],
        scratch_shapes=[pltpu.VMEM((GLA_V, GLA_K), F32), pltpu.VMEM((GLA_V, GLA_K), F32)] + 2 * [
            pltpu.VMEM((tn, GLA_K), BF16), pltpu.VMEM((tn, GLA_K), BF16), pltpu.VMEM((GLA_V, tn), BF16)],
        compiler_params=_params(("parallel", "arbitrary")),
        name="gla",
    )(qg, kg, vg, cf, tf, qg, kg, vg, cb, tb, s0f, s0b, expand)


def _outmlp_kernel(x_ref, oa_ref, ob_ref, of_ref, obw_ref, sr_ref, mod_ref, gg_ref, ones_ref, wo_ref, g2_ref,
                   w1_ref, w2_ref, o_ref, *, tf):
    x = x_ref[0]
    og = of_ref[0] + obw_ref[0]
    ss = _dot((og * og).astype(BF16), ones_ref[...])
    oc = (og * lax.rsqrt(ss * (1.0 / GLA_DV) + EPS) * gg_ref[...]) * sr_ref[0]
    c_b = NA_WIDTH
    c_c = NA_WIDTH + CONV_CH
    y = _dot(oa_ref[0], wo_ref[0:c_b, :]) + _dot(ob_ref[0], wo_ref[c_b:c_c, :]) + _dot(oc.astype(BF16), wo_ref[c_c:, :])
    x1 = x + mod_ref[0, 2:3, :] * y
    ms = jnp.mean(x1 * x1, axis=-1, keepdims=True)
    h2 = x1 * lax.rsqrt(ms + EPS) * g2_ref[...]
    h2 = (h2 * (1.0 + mod_ref[0, 4:5, :]) + mod_ref[0, 3:4, :]).astype(BF16)
    acc = jnp.zeros(x.shape, F32)
    for jf in range(w1_ref.shape[1] // tf):
        a = jnp.maximum(_dot(h2, w1_ref[:, jf * tf:(jf + 1) * tf]), 0.0)
        acc = acc + _dot((a * a).astype(BF16), w2_ref[jf * tf:(jf + 1) * tf, :])
    o_ref[0] = x1 + mod_ref[0, 5:6, :] * acc


def _outmlp(x, oa, ob, of, obw, sr, mod, lw, tm):
    b, n, d = x.shape
    dff = lw["w1"].shape[1]
    tf = min(dff, 1024)
    mod_map = (lambda bi, ti: (bi, 0, 0)) if mod.shape[0] == b else (lambda bi, ti: (0, 0, 0))
    const = lambda bi, ti: (0, 0)
    tile = lambda bi, ti: (bi, ti, 0)
    once = pl.Buffered(1)
    return pl.pallas_call(
        functools.partial(_outmlp_kernel, tf=tf),
        grid=(b, n // tm),
        in_specs=[
            pl.BlockSpec((1, tm, d), tile),
            pl.BlockSpec((1, tm, NA_WIDTH), tile),
            pl.BlockSpec((1, tm, CONV_CH), tile),
            pl.BlockSpec((1, tm, GLA_V), tile),
            pl.BlockSpec((1, tm, GLA_V), tile),
            pl.BlockSpec((1, tm, GLA_V), tile),
            pl.BlockSpec((1, 6, d), mod_map),
            pl.BlockSpec((1, GLA_V), const),
            pl.BlockSpec((GLA_V, GLA_V), const),
            pl.BlockSpec((NA_WIDTH + CONV_CH + GLA_V, d), const, pipeline_mode=once),
            pl.BlockSpec((1, d), const),
            pl.BlockSpec((d, dff), const, pipeline_mode=once),
            pl.BlockSpec((dff, d), const, pipeline_mode=once),
        ],
        out_specs=pl.BlockSpec((1, tm, d), tile),
        out_shape=jax.ShapeDtypeStruct((b, n, d), F32),
        compiler_params=_params(("parallel", "parallel")),
        name="outmlp",
    )(x, oa, ob, of, obw, sr, mod, lw["gg"], lw["ones_gla"], lw["w_out"], lw["g2"], lw["w1"], lw["w2"])


def _block_ones(n, group):
    idx = np.arange(n) // group
    return jnp.asarray(idx[:, None] == idx[None, :], BF16)


def _rope_tables(n):
    half = GLA_DK // 2
    n_freq = half // 2
    lane = np.arange(GLA_K) % GLA_DK
    use_col = lane >= half
    freq_idx = lane % n_freq
    second = (lane % half) >= n_freq
    pos = jnp.arange(n)
    inv_freq = ROPE_THETA ** (-jnp.arange(n_freq, dtype=F32) / n_freq)
    p = jnp.where(jnp.asarray(use_col)[None, :], (pos % GRID_W)[:, None], (pos // GRID_W)[:, None]).astype(F32)
    ang = p * inv_freq[jnp.asarray(freq_idx)][None, :]
    sign = jnp.where(jnp.asarray(second), 1.0, -1.0).astype(F32)
    return jnp.cos(ang), jnp.sin(ang) * sign[None, :]


def _layer_weights(l, w):
    d = w["w_in"].shape[1]
    gw = jnp.zeros((LANES, 2 * GLA_K), F32)
    gw = gw.at[:GLA_GATE_RANK, :GLA_K].set(w["gla_gw_f"][l])
    gw = gw.at[GLA_GATE_RANK:2 * GLA_GATE_RANK, GLA_K:].set(w["gla_gw_b"][l])
    row = lambda t: t.reshape(1, -1).astype(F32)
    return {
        "g1": row(w["norm1_g"][l]),
        "w_in": jnp.pad(w["w_in"][l], ((0, 0), (0, IN_WIDTH_PAD - IN_WIDTH))).astype(BF16),
        "ones_na": _block_ones(NA_WIDTH, NA_HEAD_DIM),
        "gq": row(jnp.tile(w["na_q_g"][l], NA_HEADS) * (NA_HEAD_DIM ** -0.5)),
        "gk": row(jnp.tile(w["na_k_g"][l], NA_HEADS)),
        "gw": gw.astype(BF16),
        "gb": row(jnp.concatenate([w["gla_gb_f"][l], w["gla_gb_b"][l]])),
        "conv_w": w["conv_w"][l].astype(F32),
        "conv_b": row(w["conv_b"][l]),
        "conv_ln_g": row(w["conv_ln_g"][l]),
        "conv_ln_b": row(w["conv_ln_b"][l]),
        "conv_pw_w": w["conv_pw_w"][l].astype(BF16),
        "conv_pw_b": row(w["conv_pw_b"][l]),
        "gg": row(jnp.tile(w["gla_out_g"][l], GLA_HEADS)),
        "ones_gla": _block_ones(GLA_V, GLA_DV),
        "w_out": w["w_out"][l].astype(BF16),
        "g2": row(w["norm2_g"][l]),
        "w1": w["w_mlp_in"][l].astype(BF16),
        "w2": w["w_mlp_out"][l].astype(BF16),
    }


def kernel(x, c, ctx, c_ctx, w_ada, b_ada, norm1_g, w_in, na_q_g, na_k_g, na_rpb, conv_w, conv_b, conv_ln_g,
           conv_ln_b, conv_pw_w, conv_pw_b, gla_gw_f, gla_gb_f, gla_gw_b, gla_gb_b, gla_out_g, w_out, norm2_g,
           w_mlp_in, w_mlp_out):
    w = dict(norm1_g=norm1_g, w_in=w_in, na_q_g=na_q_g, na_k_g=na_k_g, conv_w=conv_w, conv_b=conv_b,
             conv_ln_g=conv_ln_g, conv_ln_b=conv_ln_b, conv_pw_w=conv_pw_w, conv_pw_b=conv_pw_b,
             gla_gw_f=gla_gw_f, gla_gb_f=gla_gb_f, gla_gw_b=gla_gw_b, gla_gb_b=gla_gb_b, gla_out_g=gla_out_g,
             w_out=w_out, norm2_g=norm2_g, w_mlp_in=w_mlp_in, w_mlp_out=w_mlp_out)
    b, n, d = x.shape
    lc = ctx.shape[1]
    depth = w_ada.shape[0]
    n_rows = n // GRID_W
    assert n % GRID_W == 0 and n_rows % NA_TILE_ROWS == 0 and n_rows >= NA_KEY_ROWS + 1
    tm_x = 256 if n % 256 == 0 else GRID_W
    tm_c = 256 if lc % 256 == 0 else lc
    tn_x = 256 if n % 256 == 0 else GRID_W
    tn_c = 256 if lc % 256 == 0 else lc
    assert lc % GLA_INTRA_ROWS == 0

    ada_rows = -(-(b + 1) // 8) * 8
    cc = jnp.concatenate([c, c_ctx[None, :], jnp.zeros((ada_rows - b - 1, d), c.dtype)], axis=0).astype(F32)
    mod_all = _ada(cc, w_ada, b_ada)

    rope_tabs = _rope_tables(n)
    expand = jnp.asarray((np.arange(GLA_K) // GLA_DK)[:, None] == (np.arange(GLA_V) // GLA_DV)[None, :], BF16)
    zero_state = jnp.zeros((b, GLA_V, GLA_K), F32)

    xc = ctx
    for l in range(depth):
        lw = _layer_weights(l, w)
        mod = mod_all[l].reshape(ada_rows, 6, d)
        mod_x = mod[:b]
        mod_c = mod[b:b + 1]
        need_ctx_out = l < depth - 1

        (qac, kac, vac, hcc, qgc, kgc, vgc, src, cfc, tfc, cbc, tbc) = _inproj(xc, mod_c, lw, None, tm_c)
        (qa, ka, va, hc, qg, kg, vg, sr, cf, tf, cb, tb) = _inproj(x, mod_x, lw, rope_tabs, tm_x)

        ofc, obc, s_f, s_b = _gla(qgc, kgc, vgc, cfc, tfc, cbc, tbc, zero_state, zero_state, expand, tn_c)
        of, obw, _, _ = _gla(qg, kg, vg, cf, tf, cb, tb, s_f, s_b, expand, tn_x)

        bias_tab = _na_bias_table(na_rpb[l], n_rows)
        oa = _na_attn(qa, ka, va, kac, vac, bias_tab, n_rows)
        ob = _conv(hc, lw)
        x = _outmlp(x, oa, ob, of, obw, sr, mod_x, lw, tm_x)

        if need_ctx_out:
            oac = _ctx_attn(qac, kac, vac)
            obc_conv = _conv(hcc, lw)
            xc = _outmlp(xc, oac, obc_conv, ofc, obc, src, mod_c, lw, tm_c)
    return x
```

```python
import functools

import numpy as np
import jax
import jax.numpy as jnp
from jax import lax
from jax.experimental import pallas as pl
from jax.experimental.pallas import tpu as pltpu

F32 = jnp.float32
BF16 = jnp.bfloat16

GRID_W = 64
EPS = 1e-6

NA_HEADS = 8
NA_HEAD_DIM = 64
NA_WIN_R = 8
NA_WIN_C = 16
NA_WIDTH = NA_HEADS * NA_HEAD_DIM

CONV_CH = 256
CONV_WIDTH = 31

GLA_HEADS = 4
GLA_DK = 32
GLA_DV = 64
GLA_GATE_RANK = 16
GLA_TAU = 16.0
GLA_K = GLA_HEADS * GLA_DK
GLA_V = GLA_HEADS * GLA_DV
ROPE_THETA = 10000.0

_C_QA, _C_KA, _C_VA, _C_UB = 0, NA_WIDTH, 2 * NA_WIDTH, 3 * NA_WIDTH
_C_QG = _C_UB + 2 * CONV_CH
_C_KG = _C_QG + GLA_K
_C_VG = _C_KG + GLA_K
_C_RG = _C_VG + GLA_V
_C_Z = _C_RG + GLA_V
IN_WIDTH = _C_Z + 2 * GLA_GATE_RANK
LANES = 128
IN_WIDTH_PAD = -(-IN_WIDTH // LANES) * LANES

GLA_SUB = 16
NA_TILE_ROWS = 4
NA_KEY_ROWS = NA_TILE_ROWS + NA_WIN_R - 1
MASK_NEG = -1e30
LOG2E = 1.4426950408889634
VMEM_LIMIT = 56 * 1024 * 1024


def _params(sem, vmem=VMEM_LIMIT):
    return pltpu.CompilerParams(dimension_semantics=sem, vmem_limit_bytes=vmem)


def _silu(t):
    return t * jax.nn.sigmoid(t)


def _dot(a, b):
    return jnp.dot(a, b, preferred_element_type=F32)


def _dot_nt(a, b):
    return lax.dot_general(a, b, (((1,), (1,)), ((), ())), preferred_element_type=F32)


def _ada_kernel(c_ref, w_ref, b_ref, o_ref):
    s = _silu(c_ref[...]).astype(BF16)
    o_ref[0] = _dot(s, w_ref[0].astype(BF16)) + b_ref[0]


def _ada(cc, w_ada, b_ada):
    depth, d, d6 = w_ada.shape
    rows = cc.shape[0]
    tn = 1536 if d6 % 1536 == 0 else d6
    return pl.pallas_call(
        _ada_kernel,
        grid=(depth, d6 // tn),
        in_specs=[
            pl.BlockSpec((rows, d), lambda l, n: (0, 0)),
            pl.BlockSpec((1, d, tn), lambda l, n: (l, 0, n)),
            pl.BlockSpec((1, 1, tn), lambda l, n: (l, 0, n)),
        ],
        out_specs=pl.BlockSpec((1, rows, tn), lambda l, n: (l, 0, n)),
        out_shape=jax.ShapeDtypeStruct((depth, rows, d6), F32),
        compiler_params=_params(("parallel", "parallel")),
        name="ada",
    )(cc, w_ada, b_ada.reshape(depth, 1, d6))


def _block_scan(t, rowm, reverse):
    n = t.shape[0]
    s = 1
    while s < GLA_SUB:
        if reverse:
            t = t + jnp.where(rowm < GLA_SUB - s, pltpu.roll(t, n - s, 0), 0.0)
        else:
            t = t + jnp.where(rowm >= s, pltpu.roll(t, s, 0), 0.0)
        s *= 2
    return t


def _inproj_kernel(*refs, rope):
    if rope:
        (x_ref, mod_ref, g1_ref, w_ref, ones_ref, gq_ref, gk_ref, gw_ref, gb_ref, cos_ref, sin_ref,
         qa_ref, ka_ref, va_ref, hc_ref, qg_ref, kg_ref, vg_ref, sr_ref, cf_ref, tf_ref, cb_ref, tb_ref) = refs
    else:
        (x_ref, mod_ref, g1_ref, w_ref, ones_ref, gq_ref, gk_ref, gw_ref, gb_ref,
         qa_ref, ka_ref, va_ref, hc_ref, qg_ref, kg_ref, vg_ref, sr_ref, cf_ref, tf_ref, cb_ref, tb_ref) = refs
    x = x_ref[0]
    ms = jnp.mean(x * x, axis=-1, keepdims=True)
    h = x * lax.rsqrt(ms + EPS) * g1_ref[...]
    hb = (h * (1.0 + mod_ref[0, 1:2, :]) + mod_ref[0, 0:1, :]).astype(BF16)

    def seg(lo, hi):
        return _dot(hb, w_ref[:, lo:hi])

    def head_norm(t, g_ref):
        ss = _dot((t * t).astype(BF16), ones_ref[...])
        return (t * lax.rsqrt(ss * (1.0 / NA_HEAD_DIM) + EPS) * g_ref[...]).astype(BF16)

    qa_ref[0] = head_norm(seg(_C_QA, _C_KA), gq_ref)
    ka_ref[0] = head_norm(seg(_C_KA, _C_VA), gk_ref)
    va_ref[0] = seg(_C_VA, _C_UB).astype(BF16)

    ub = seg(_C_UB, _C_QG)
    hc_ref[0] = ub[:, :CONV_CH] * jax.nn.sigmoid(ub[:, CONV_CH:])

    qkg = seg(_C_QG, _C_VG)
    qg = qkg[:, :GLA_K]
    kg = qkg[:, GLA_K:]
    if rope:
        cos = cos_ref[...]
        sin = sin_ref[...]
        first = (lax.broadcasted_iota(jnp.int32, (1, GLA_K), 1) & 15) < 8

        def rot(t):
            partner = jnp.where(first, pltpu.roll(t, GLA_K - 8, 1), pltpu.roll(t, 8, 1))
            return t * cos + partner * sin

        qg = rot(qg)
        kg = rot(kg)
    qg_ref[0] = qg * (GLA_DK ** -0.5)
    kg_ref[0] = kg
    vg_ref[0] = seg(_C_VG, _C_RG)
    sr_ref[0] = _silu(seg(_C_RG, _C_Z))

    z = seg(_C_Z, IN_WIDTH_PAD).astype(BF16)
    a = _dot(z, gw_ref[...]) + gb_ref[...]
    la = (jnp.minimum(a, 0.0) - jnp.log(1.0 + jnp.exp(-jnp.abs(a)))) * (LOG2E / GLA_TAU)
    laf = la[:, :GLA_K]
    lab = la[:, GLA_K:]
    rowm = lax.broadcasted_iota(jnp.int32, laf.shape, 0) & (GLA_SUB - 1)
    pf = _block_scan(laf, rowm, False)
    sf = _block_scan(laf, rowm, True)
    pb = _block_scan(lab, rowm, False)
    sb = _block_scan(lab, rowm, True)
    cf_ref[0] = pf
    tf_ref[0] = pf + sf - laf
    cb_ref[0] = sb
    tb_ref[0] = pb + sb - lab


def _inproj(x, mod, lw, rope_tabs, tm):
    b, n, d = x.shape
    nt = n // tm
    rope = rope_tabs is not None
    mod_map = (lambda bi, ti: (bi, 0, 0)) if mod.shape[0] == b else (lambda bi, ti: (0, 0, 0))
    const = lambda bi, ti: (0, 0)
    tile = lambda bi, ti: (bi, ti, 0)
    in_specs = [
        pl.BlockSpec((1, tm, d), tile),
        pl.BlockSpec((1, 6, d), mod_map),
        pl.BlockSpec((1, d), const),
        pl.BlockSpec((d, IN_WIDTH_PAD), const),
        pl.BlockSpec((NA_WIDTH, NA_WIDTH), const),
        pl.BlockSpec((1, NA_WIDTH), const),
        pl.BlockSpec((1, NA_WIDTH), const),
        pl.BlockSpec((LANES, 2 * GLA_K), const),
        pl.BlockSpec((1, 2 * GLA_K), const),
    ]
    args = [x, mod, lw["g1"], lw["w_in"], lw["ones_na"], lw["gq"], lw["gk"], lw["gw"], lw["gb"]]
    if rope:
        in_specs += [pl.BlockSpec((tm, GLA_K), lambda bi, ti: (ti, 0))] * 2
        args += list(rope_tabs)
    widths = [(NA_WIDTH, BF16)] * 3 + [(CONV_CH, F32), (GLA_K, F32), (GLA_K, F32), (GLA_V, F32), (GLA_V, F32)] + [
        (GLA_K, F32)] * 4
    out_specs = [pl.BlockSpec((1, tm, w), tile) for w, _ in widths]
    out_shape = [jax.ShapeDtypeStruct((b, n, w), dt) for w, dt in widths]
    return pl.pallas_call(
        functools.partial(_inproj_kernel, rope=rope),
        grid=(b, nt),
        in_specs=in_specs,
        out_specs=out_specs,
        out_shape=out_shape,
        compiler_params=_params(("parallel", "parallel")),
        name="inproj_rope" if rope else "inproj",
    )(*args)


def _softmax_pv(s_list, v_list):
    m = s_list[0].max(axis=-1, keepdims=True)
    for s in s_list[1:]:
        m = jnp.maximum(m, s.max(axis=-1, keepdims=True))
    l = 0.0
    o = 0.0
    for s, v in zip(s_list, v_list):
        p = jnp.exp2(s - m)
        l = l + p.sum(axis=-1, keepdims=True)
        o = o + _dot(p.astype(BF16), v)
    return o / l


def _na_kernel(q_ref, k_ref, v_ref, kc_ref, vc_ref, bias_ref, o_ref, *, n_rows):
    j = pl.program_id(1)
    w0 = jnp.clip(NA_TILE_ROWS * j - NA_WIN_R // 2, 0, n_rows - NA_KEY_ROWS)
    start = pl.multiple_of(w0 * GRID_W, GRID_W)
    nk = NA_KEY_ROWS * GRID_W
    low = lax.broadcasted_iota(jnp.int32, (1, LANES), 1) < NA_HEAD_DIM
    for p in range(NA_WIDTH // LANES):
        lanes = slice(p * LANES, (p + 1) * LANES)
        qp = q_ref[0, :, lanes]
        kp = k_ref[0, pl.ds(start, nk), lanes]
        vp = v_ref[0, pl.ds(start, nk), lanes]
        kcp = kc_ref[0, :, lanes]
        vcp = vc_ref[0, :, lanes]
        outs = []
        for e in range(2):
            qe = jnp.where(low if e == 0 else jnp.logical_not(low), qp, jnp.zeros_like(qp))
            s_loc = _dot_nt(qe, kp) + bias_ref[0, 2 * p + e]
            s_ctx = _dot_nt(qe, kcp)
            outs.append(_softmax_pv([s_loc, s_ctx], [vp, vcp]))
        o_ref[0, :, lanes] = jnp.where(low, outs[0], outs[1]).astype(BF16)


def _na_bias_table(rpb, n_rows):
    n_tiles = n_rows // NA_TILE_ROWS
    cq = np.arange(GRID_W)
    cs = np.clip(cq - NA_WIN_C // 2, 0, GRID_W - NA_WIN_C)
    ck = np.arange(GRID_W)[None, :]
    col_valid = (ck >= cs[:, None]) & (ck < cs[:, None] + NA_WIN_C)
    padded = jnp.pad(rpb.astype(F32), ((0, 0), (0, 0), (GRID_W, GRID_W)))
    rows_q = [padded[:, :, GRID_W + NA_WIN_C - 1 - q:GRID_W + NA_WIN_C - 1 - q + GRID_W] for q in range(GRID_W)]
    blocks = jnp.stack(rows_q, axis=2)
    blocks = jnp.where(jnp.asarray(col_valid)[None, None], blocks * LOG2E, MASK_NEG)
    masked_block = jnp.full((NA_HEADS, GRID_W, GRID_W), MASK_NEG, F32)
    pats = []
    for jt in (0, 1, n_tiles - 1):
        rq0 = jt * NA_TILE_ROWS
        w0 = int(np.clip(rq0 - NA_WIN_R // 2, 0, n_rows - NA_KEY_ROWS))
        q_rows = []
        for ri in range(NA_TILE_ROWS):
            r = rq0 + ri
            r0 = int(np.clip(r - NA_WIN_R // 2, 0, n_rows - NA_WIN_R))
            k_blocks = []
            for rki in range(NA_KEY_ROWS):
                rk = w0 + rki
                k_blocks.append(blocks[:, rk - r + NA_WIN_R - 1] if r0 <= rk < r0 + NA_WIN_R else masked_block)
            q_rows.append(jnp.concatenate(k_blocks, axis=-1))
        pats.append(jnp.concatenate(q_rows, axis=-2))
    return jnp.stack(pats)


def _na_attn(qa, ka, va, kc, vc, bias_tab, n_rows):
    b, n, _ = qa.shape
    lc = kc.shape[1]
    n_tiles = n_rows // NA_TILE_ROWS
    tq = NA_TILE_ROWS * GRID_W
    nk = NA_KEY_ROWS * GRID_W

    def bias_map(bi, j):
        return (jnp.where(j == 0, 0, jnp.where(j == n_tiles - 1, 2, 1)), 0, 0, 0)

    return pl.pallas_call(
        functools.partial(_na_kernel, n_rows=n_rows),
        grid=(b, n_tiles),
        in_specs=[
            pl.BlockSpec((1, tq, NA_WIDTH), lambda bi, j: (bi, j, 0)),
            pl.BlockSpec((1, n, NA_WIDTH), lambda bi, j: (bi, 0, 0)),
            pl.BlockSpec((1, n, NA_WIDTH), lambda bi, j: (bi, 0, 0)),
            pl.BlockSpec((1, lc, NA_WIDTH), lambda bi, j: (bi, 0, 0)),
            pl.BlockSpec((1, lc, NA_WIDTH), lambda bi, j: (bi, 0, 0)),
            pl.BlockSpec((1, NA_HEADS, tq, nk), bias_map),
        ],
        out_specs=pl.BlockSpec((1, tq, NA_WIDTH), lambda bi, j: (bi, j, 0)),
        out_shape=jax.ShapeDtypeStruct((b, n, NA_WIDTH), BF16),
        compiler_params=_params(("parallel", "arbitrary")),
        name="na_attn",
    )(qa, ka, va, kc, vc, bias_tab)


def _ctx_attn_kernel(q_ref, k_ref, v_ref, o_ref):
    low = lax.broadcasted_iota(jnp.int32, (1, LANES), 1) < NA_HEAD_DIM
    for p in range(NA_WIDTH // LANES):
        lanes = slice(p * LANES, (p + 1) * LANES)
        qp = q_ref[0, :, lanes]
        kp = k_ref[0, :, lanes]
        vp = v_ref[0, :, lanes]
        outs = []
        for e in range(2):
            qe = jnp.where(low if e == 0 else jnp.logical_not(low), qp, jnp.zeros_like(qp))
            outs.append(_softmax_pv([_dot_nt(qe, kp)], [vp]))
        o_ref[0, :, lanes] = jnp.where(low, outs[0], outs[1]).astype(BF16)


def _ctx_attn(q, k, v):
    b, lc, _ = q.shape
    spec = pl.BlockSpec((1, lc, NA_WIDTH), lambda bi: (bi, 0, 0))
    return pl.pallas_call(
        _ctx_attn_kernel,
        grid=(b,),
        in_specs=[spec, spec, spec],
        out_specs=spec,
        out_shape=jax.ShapeDtypeStruct((b, lc, NA_WIDTH), BF16),
        compiler_params=_params(("parallel",)),
        name="ctx_attn",
    )(q, k, v)


CONV_HALO = 16


def _conv_kernel(h_ref, w_ref, cb_ref, lg_ref, lb_ref, pw_ref, pb_ref, o_ref, hp_ref, *, n, tile):
    hp_ref[0:CONV_HALO, :] = jnp.zeros((CONV_HALO, CONV_CH), F32)
    hp_ref[CONV_HALO + n:, :] = jnp.zeros((CONV_HALO, CONV_CH), F32)
    hp_ref[CONV_HALO:CONV_HALO + n, :] = h_ref[0]
    shift = CONV_HALO - CONV_WIDTH // 2

    def body(i, carry):
        t0 = pl.multiple_of(i * tile, tile)
        chunk = hp_ref[pl.ds(t0, tile + 2 * CONV_HALO), :]
        acc = jnp.zeros((tile, CONV_CH), F32)
        rows = tile + 2 * CONV_HALO
        for phase in range(8):
            off = (phase + shift) % 8
            base = phase + shift - off
            shifted = pltpu.roll(chunk, rows - off, 0) if off else chunk
            shifted = shifted[base:base + tile + 8 * ((CONV_WIDTH - 1) // 8), :]
            for k in range(phase, CONV_WIDTH, 8):
                acc = acc + shifted[k - phase:k - phase + tile, :] * w_ref[k:k + 1, :]
        hcv = acc + cb_ref[...]
        mu = jnp.mean(hcv, axis=-1, keepdims=True)
        ctr = hcv - mu
        var = jnp.mean(ctr * ctr, axis=-1, keepdims=True)
        y = _silu(ctr * lax.rsqrt(var + EPS) * lg_ref[...] + lb_ref[...])
        o_ref[0, pl.ds(t0, tile), :] = (_dot(y.astype(BF16), pw_ref[...]) + pb_ref[...]).astype(BF16)
        return carry

    lax.fori_loop(0, n // tile, body, 0)


def _conv(hc, lw):
    b, n, _ = hc.shape
    tile = 128
    const = lambda bi: (0, 0)
    vec = pl.BlockSpec((1, CONV_CH), const)
    return pl.pallas_call(
        functools.partial(_conv_kernel, n=n, tile=tile),
        grid=(b,),
        in_specs=[
            pl.BlockSpec((1, n, CONV_CH), lambda bi: (bi, 0, 0)),
            pl.BlockSpec((CONV_WIDTH, CONV_CH), const),
            vec, vec, vec,
            pl.BlockSpec((CONV_CH, CONV_CH), const),
            vec,
        ],
        out_specs=pl.BlockSpec((1, n, CONV_CH), lambda bi: (bi, 0, 0)),
        out_shape=jax.ShapeDtypeStruct((b, n, CONV_CH), BF16),
        scratch_shapes=[pltpu.VMEM((n + 2 * CONV_HALO, CONV_CH), F32)],
        compiler_params=_params(("parallel",)),
        name="conv",
    )(hc, lw["conv_w"], lw["conv_b"], lw["conv_ln_g"], lw["conv_ln_b"], lw["conv_pw_w"], lw["conv_pw_b"])


GLA_INTRA_ROWS = 64


def _gla_intra(q_ref, k_ref, v_ref, c_ref, exp_ref, o_ref, r0, rowm, forward):
    sub = GLA_INTRA_ROWS
    q = q_ref[0, pl.ds(r0, sub), :]
    k = k_ref[0, pl.ds(r0, sub), :]
    v = v_ref[0, pl.ds(r0, sub), :]
    cum = c_ref[0, pl.ds(r0, sub), :]
    acc = _dot((q * k).astype(BF16), exp_ref[...]) * v

    def shift(t, d):
        return pltpu.roll(t, d if forward else sub - d, 0)

    for b in range(8):
        kb, cb, vb = (shift(k, b), shift(cum, b), shift(v, b)) if b else (k, cum, v)
        for d in (b, 8 + b):
            if d == 0:
                continue
            kd, cd, vd = (shift(kb, 8), shift(cb, 8), shift(vb, 8)) if d >= 8 else (kb, cb, vb)
            valid = (rowm >= d) if forward else (rowm < GLA_SUB - d)
            pd = q * kd * jnp.exp2(jnp.where(valid, cum - cd, MASK_NEG))
            acc = acc + _dot(pd.astype(BF16), exp_ref[...]) * vd
    o_ref[0, pl.ds(r0, sub), :] = acc


def _gla_prepare(q_ref, k_ref, v_ref, c_ref, t_ref, qt_ref, kh_ref, vt_ref):
    cum = c_ref[0]
    qt_ref[...] = (q_ref[0] * jnp.exp2(cum)).astype(BF16)
    kh_ref[...] = (k_ref[0] * jnp.exp2(t_ref[0] - cum)).astype(BF16)
    vt_ref[...] = v_ref[0].T.astype(BF16)


def _gla_updates(ib, vt_ref, kh_ref, u_ref, chunk_in_group):
    per_pass = GLA_INTRA_ROWS // GLA_SUB
    g0 = pl.multiple_of(((ib * GLA_INTRA_ROWS) // LANES) * LANES, LANES)
    first = (ib * per_pass) % (LANES // GLA_SUB)
    vtg = vt_ref[:, pl.ds(g0, LANES)]
    khg = kh_ref[pl.ds(g0, LANES), :]
    for c in range(per_pass):
        u_ref[ib * per_pass + c] = _dot(jnp.where(chunk_in_group == first + c, vtg, jnp.zeros((), BF16)), khg)


def _gla_recur(i, t_ref, s_ref, u_ref, sbf_ref, same_head):
    state = s_ref[...]
    sbf_ref[i] = jnp.where(same_head, state, 0.0).astype(BF16)
    r0 = pl.multiple_of(i * GLA_SUB, GLA_SUB)
    s_ref[...] = jnp.exp2(t_ref[0, pl.ds(r0, 1), :]) * state + u_ref[i]


def _gla_kernel(qf_ref, kf_ref, vf_ref, cf_ref, tf_ref, qb_ref, kb_ref, vb_ref, cb_ref, tb_ref, s0f_ref, s0b_ref,
                exp_ref, of_ref, ob_ref, sfo_ref, sbo_ref, sf_ref, sb_ref, qtf_ref, khf_ref, vtf_ref, uf_ref,
                sbff_ref, qtb_ref, khb_ref, vtb_ref, ub_ref, sbfb_ref):
    j = pl.program_id(1)
    tn = qf_ref.shape[1]
    n_chunks = tn // GLA_SUB

    @pl.when(j == 0)
    def _():
        sf_ref[...] = s0f_ref[0]
        sb_ref[...] = s0b_ref[0]

    _gla_prepare(qf_ref, kf_ref, vf_ref, cf_ref, tf_ref, qtf_ref, khf_ref, vtf_ref)
    _gla_prepare(qb_ref, kb_ref, vb_ref, cb_ref, tb_ref, qtb_ref, khb_ref, vtb_ref)

    rowm = lax.broadcasted_iota(jnp.int32, (GLA_INTRA_ROWS, GLA_K), 0) & (GLA_SUB - 1)
    chunk_in_group = lax.broadcasted_iota(jnp.int32, (1, LANES), 1) // GLA_SUB

    def intra(ib, carry):
        r0 = pl.multiple_of(ib * GLA_INTRA_ROWS, GLA_INTRA_ROWS)
        _gla_intra(qf_ref, kf_ref, vf_ref, cf_ref, exp_ref, of_ref, r0, rowm, True)
        _gla_intra(qb_ref, kb_ref, vb_ref, cb_ref, exp_ref, ob_ref, r0, rowm, False)
        _gla_updates(ib, vtf_ref, khf_ref, uf_ref, chunk_in_group)
        _gla_updates(ib, vtb_ref, khb_ref, ub_ref, chunk_in_group)
        return carry

    lax.fori_loop(0, tn // GLA_INTRA_ROWS, intra, 0)

    head_of_row = lax.broadcasted_iota(jnp.int32, (GLA_V, GLA_K), 0) // GLA_DV
    head_of_col = lax.broadcasted_iota(jnp.int32, (GLA_V, GLA_K), 1) // GLA_DK
    same_head = head_of_row == head_of_col

    def recur(i, carry):
        _gla_recur(i, tf_ref, sf_ref, uf_ref, sbff_ref, same_head)
        _gla_recur(n_chunks - 1 - i, tb_ref, sb_ref, ub_ref, sbfb_ref, same_head)
        return carry

    lax.fori_loop(0, n_chunks, recur, 0)

    for i in range(n_chunks):
        rows = slice(i * GLA_SUB, (i + 1) * GLA_SUB)
        of_ref[0, rows, :] = of_ref[0, rows, :] + _dot_nt(qtf_ref[rows, :], sbff_ref[i])
        ob_ref[0, rows, :] = ob_ref[0, rows, :] + _dot_nt(qtb_ref[rows, :], sbfb_ref[i])

    @pl.when(j == pl.num_programs(1) - 1)
    def _():
        sfo_ref[0] = sf_ref[...]
        sbo_ref[0] = sb_ref[...]


def _gla(qg, kg, vg, cf, tf, cb, tb, s0f, s0b, expand, tn):
    b, n, _ = qg.shape
    nt = n // tn
    fwd = lambda bi, j: (bi, j, 0)
    bwd = lambda bi, j: (bi, nt - 1 - j, 0)
    per_b = lambda bi, j: (bi, 0, 0)

    def spec(w, m):
        return pl.BlockSpec((1, tn, w), m)

    state_spec = pl.BlockSpec((1, GLA_V, GLA_K), per_b)
    return pl.pallas_call(
        _gla_kernel,
        grid=(b, nt),
        in_specs=[spec(GLA_K, fwd), spec(GLA_K, fwd), spec(GLA_V, fwd), spec(GLA_K, fwd), spec(GLA_K, fwd),
                  spec(GLA_K, bwd), spec(GLA_K, bwd), spec(GLA_V, bwd), spec(GLA_K, bwd), spec(GLA_K, bwd),
                  state_spec, state_spec,
                  pl.BlockSpec((GLA_K, GLA_V), lambda bi, j: (0, 0))],
        out_specs=[spec(GLA_V, fwd), spec(GLA_V, bwd), state_spec, state_spec],
        out_shape=[jax.ShapeDtypeStruct((b, n, GLA_V), F32), jax.ShapeDtypeStruct((b, n, GLA_V), F32),
                   jax.ShapeDtypeStruct((b, GLA_V, GLA_K), F32), jax.ShapeDtypeStruct((b, GLA_V, GLA_K), F32)],
        scratch_shapes=[pltpu.VMEM((GLA_V, GLA_K), F32), pltpu.VMEM((GLA_V, GLA_K), F32)] + 2 * [
            pltpu.VMEM((tn, GLA_K), BF16), pltpu.VMEM((tn, GLA_K), BF16), pltpu.VMEM((GLA_V, tn), BF16),
            pltpu.VMEM((tn // GLA_SUB, GLA_V, GLA_K), F32), pltpu.VMEM((tn // GLA_SUB, GLA_V, GLA_K), BF16)],
        compiler_params=_params(("parallel", "arbitrary")),
        name="gla",
    )(qg, kg, vg, cf, tf, qg, kg, vg, cb, tb, s0f, s0b, expand)


def _outmlp_kernel(x_ref, oa_ref, ob_ref, of_ref, obw_ref, sr_ref, mod_ref, gg_ref, ones_ref, wo_ref, g2_ref,
                   w1_ref, w2_ref, o_ref, *, tf):
    x = x_ref[0]
    og = of_ref[0] + obw_ref[0]
    ss = _dot((og * og).astype(BF16), ones_ref[...])
    oc = (og * lax.rsqrt(ss * (1.0 / GLA_DV) + EPS) * gg_ref[...]) * sr_ref[0]
    c_b = NA_WIDTH
    c_c = NA_WIDTH + CONV_CH
    y = _dot(oa_ref[0], wo_ref[0:c_b, :]) + _dot(ob_ref[0], wo_ref[c_b:c_c, :]) + _dot(oc.astype(BF16), wo_ref[c_c:, :])
    x1 = x + mod_ref[0, 2:3, :] * y
    ms = jnp.mean(x1 * x1, axis=-1, keepdims=True)
    h2 = x1 * lax.rsqrt(ms + EPS) * g2_ref[...]
    h2 = (h2 * (1.0 + mod_ref[0, 4:5, :]) + mod_ref[0, 3:4, :]).astype(BF16)
    acc = jnp.zeros(x.shape, F32)
    for jf in range(w1_ref.shape[1] // tf):
        a = jnp.maximum(_dot(h2, w1_ref[:, jf * tf:(jf + 1) * tf]), 0.0)
        acc = acc + _dot((a * a).astype(BF16), w2_ref[jf * tf:(jf + 1) * tf, :])
    o_ref[0] = x1 + mod_ref[0, 5:6, :] * acc


def _outmlp(x, oa, ob, of, obw, sr, mod, lw, tm):
    b, n, d = x.shape
    dff = lw["w1"].shape[1]
    tf = min(dff, 1024)
    mod_map = (lambda bi, ti: (bi, 0, 0)) if mod.shape[0] == b else (lambda bi, ti: (0, 0, 0))
    const = lambda bi, ti: (0, 0)
    tile = lambda bi, ti: (bi, ti, 0)
    once = pl.Buffered(1)
    return pl.pallas_call(
        functools.partial(_outmlp_kernel, tf=tf),
        grid=(b, n // tm),
        in_specs=[
            pl.BlockSpec((1, tm, d), tile),
            pl.BlockSpec((1, tm, NA_WIDTH), tile),
            pl.BlockSpec((1, tm, CONV_CH), tile),
            pl.BlockSpec((1, tm, GLA_V), tile),
            pl.BlockSpec((1, tm, GLA_V), tile),
            pl.BlockSpec((1, tm, GLA_V), tile),
            pl.BlockSpec((1, 6, d), mod_map),
            pl.BlockSpec((1, GLA_V), const),
            pl.BlockSpec((GLA_V, GLA_V), const),
            pl.BlockSpec((NA_WIDTH + CONV_CH + GLA_V, d), const, pipeline_mode=once),
            pl.BlockSpec((1, d), const),
            pl.BlockSpec((d, dff), const, pipeline_mode=once),
            pl.BlockSpec((dff, d), const, pipeline_mode=once),
        ],
        out_specs=pl.BlockSpec((1, tm, d), tile),
        out_shape=jax.ShapeDtypeStruct((b, n, d), F32),
        compiler_params=_params(("parallel", "parallel")),
        name="outmlp",
    )(x, oa, ob, of, obw, sr, mod, lw["gg"], lw["ones_gla"], lw["w_out"], lw["g2"], lw["w1"], lw["w2"])


def _block_ones(n, group):
    idx = np.arange(n) // group
    return jnp.asarray(idx[:, None] == idx[None, :], BF16)


def _rope_tables(n):
    half = GLA_DK // 2
    n_freq = half // 2
    lane = np.arange(GLA_K) % GLA_DK
    use_col = lane >= half
    freq_idx = lane % n_freq
    second = (lane % half) >= n_freq
    pos = jnp.arange(n)
    inv_freq = ROPE_THETA ** (-jnp.arange(n_freq, dtype=F32) / n_freq)
    p = jnp.where(jnp.asarray(use_col)[None, :], (pos % GRID_W)[:, None], (pos // GRID_W)[:, None]).astype(F32)
    ang = p * inv_freq[jnp.asarray(freq_idx)][None, :]
    sign = jnp.where(jnp.asarray(second), 1.0, -1.0).astype(F32)
    return jnp.cos(ang), jnp.sin(ang) * sign[None, :]


def _layer_weights(l, w):
    d = w["w_in"].shape[1]
    gw = jnp.zeros((LANES, 2 * GLA_K), F32)
    gw = gw.at[:GLA_GATE_RANK, :GLA_K].set(w["gla_gw_f"][l])
    gw = gw.at[GLA_GATE_RANK:2 * GLA_GATE_RANK, GLA_K:].set(w["gla_gw_b"][l])
    row = lambda t: t.reshape(1, -1).astype(F32)
    return {
        "g1": row(w["norm1_g"][l]),
        "w_in": jnp.pad(w["w_in"][l], ((0, 0), (0, IN_WIDTH_PAD - IN_WIDTH))).astype(BF16),
        "ones_na": _block_ones(NA_WIDTH, NA_HEAD_DIM),
        "gq": row(jnp.tile(w["na_q_g"][l], NA_HEADS) * (NA_HEAD_DIM ** -0.5 * LOG2E)),
        "gk": row(jnp.tile(w["na_k_g"][l], NA_HEADS)),
        "gw": gw.astype(BF16),
        "gb": row(jnp.concatenate([w["gla_gb_f"][l], w["gla_gb_b"][l]])),
        "conv_w": w["conv_w"][l].astype(F32),
        "conv_b": row(w["conv_b"][l]),
        "conv_ln_g": row(w["conv_ln_g"][l]),
        "conv_ln_b": row(w["conv_ln_b"][l]),
        "conv_pw_w": w["conv_pw_w"][l].astype(BF16),
        "conv_pw_b": row(w["conv_pw_b"][l]),
        "gg": row(jnp.tile(w["gla_out_g"][l], GLA_HEADS)),
        "ones_gla": _block_ones(GLA_V, GLA_DV),
        "w_out": w["w_out"][l].astype(BF16),
        "g2": row(w["norm2_g"][l]),
        "w1": w["w_mlp_in"][l].astype(BF16),
        "w2": w["w_mlp_out"][l].astype(BF16),
    }


def kernel(x, c, ctx, c_ctx, w_ada, b_ada, norm1_g, w_in, na_q_g, na_k_g, na_rpb, conv_w, conv_b, conv_ln_g,
           conv_ln_b, conv_pw_w, conv_pw_b, gla_gw_f, gla_gb_f, gla_gw_b, gla_gb_b, gla_out_g, w_out, norm2_g,
           w_mlp_in, w_mlp_out):
    w = dict(norm1_g=norm1_g, w_in=w_in, na_q_g=na_q_g, na_k_g=na_k_g, conv_w=conv_w, conv_b=conv_b,
             conv_ln_g=conv_ln_g, conv_ln_b=conv_ln_b, conv_pw_w=conv_pw_w, conv_pw_b=conv_pw_b,
             gla_gw_f=gla_gw_f, gla_gb_f=gla_gb_f, gla_gw_b=gla_gw_b, gla_gb_b=gla_gb_b, gla_out_g=gla_out_g,
             w_out=w_out, norm2_g=norm2_g, w_mlp_in=w_mlp_in, w_mlp_out=w_mlp_out)
    b, n, d = x.shape
    lc = ctx.shape[1]
    depth = w_ada.shape[0]
    n_rows = n // GRID_W
    assert n % GRID_W == 0 and n_rows % NA_TILE_ROWS == 0 and n_rows >= NA_KEY_ROWS + 1
    tm_x = tm_c = tn_x = tn_c = 256
    assert n % tm_x == 0 and lc % tm_c == 0

    ada_rows = -(-(b + 1) // 8) * 8
    cc = jnp.concatenate([c, c_ctx[None, :], jnp.zeros((ada_rows - b - 1, d), c.dtype)], axis=0).astype(F32)
    mod_all = _ada(cc, w_ada, b_ada)

    rope_tabs = _rope_tables(n)
    expand = jnp.asarray((np.arange(GLA_K) // GLA_DK)[:, None] == (np.arange(GLA_V) // GLA_DV)[None, :], BF16)
    zero_state = jnp.zeros((b, GLA_V, GLA_K), F32)

    xc = ctx
    for l in range(depth):
        lw = _layer_weights(l, w)
        mod = mod_all[l].reshape(ada_rows, 6, d)
        mod_x = mod[:b]
        mod_c = mod[b:b + 1]
        need_ctx_out = l < depth - 1

        (qac, kac, vac, hcc, qgc, kgc, vgc, src, cfc, tfc, cbc, tbc) = _inproj(xc, mod_c, lw, None, tm_c)
        (qa, ka, va, hc, qg, kg, vg, sr, cf, tf, cb, tb) = _inproj(x, mod_x, lw, rope_tabs, tm_x)

        ofc, obc, s_f, s_b = _gla(qgc, kgc, vgc, cfc, tfc, cbc, tbc, zero_state, zero_state, expand, tn_c)
        of, obw, _, _ = _gla(qg, kg, vg, cf, tf, cb, tb, s_f, s_b, expand, tn_x)

        bias_tab = _na_bias_table(na_rpb[l], n_rows)
        oa = _na_attn(qa, ka, va, kac, vac, bias_tab, n_rows)
        ob = _conv(hc, lw)
        x = _outmlp(x, oa, ob, of, obw, sr, mod_x, lw, tm_x)

        if need_ctx_out:
            oac = _ctx_attn(qac, kac, vac)
            obc_conv = _conv(hcc, lw)
            xc = _outmlp(xc, oac, obc_conv, ofc, obc, src, mod_c, lw, tm_c)
    return x
```

```python
import functools

import numpy as np
import jax
import jax.numpy as jnp
from jax import lax
from jax.experimental import pallas as pl
from jax.experimental.pallas import tpu as pltpu

F32 = jnp.float32
BF16 = jnp.bfloat16

GRID_W = 64
EPS = 1e-6

NA_HEADS = 8
NA_HEAD_DIM = 64
NA_WIN_R = 8
NA_WIN_C = 16
NA_WIDTH = NA_HEADS * NA_HEAD_DIM

CONV_CH = 256
CONV_WIDTH = 31

GLA_HEADS = 4
GLA_DK = 32
GLA_DV = 64
GLA_GATE_RANK = 16
GLA_TAU = 16.0
GLA_K = GLA_HEADS * GLA_DK
GLA_V = GLA_HEADS * GLA_DV
ROPE_THETA = 10000.0

_C_QA, _C_KA, _C_VA, _C_UB = 0, NA_WIDTH, 2 * NA_WIDTH, 3 * NA_WIDTH
_C_QG = _C_UB + 2 * CONV_CH
_C_KG = _C_QG + GLA_K
_C_VG = _C_KG + GLA_K
_C_RG = _C_VG + GLA_V
_C_Z = _C_RG + GLA_V
IN_WIDTH = _C_Z + 2 * GLA_GATE_RANK
LANES = 128
IN_WIDTH_PAD = -(-IN_WIDTH // LANES) * LANES

GLA_SUB = 16
NA_TILE_ROWS = 4
NA_KEY_ROWS = NA_TILE_ROWS + NA_WIN_R - 1
MASK_NEG = -1e30
LOG2E = 1.4426950408889634
VMEM_LIMIT = 56 * 1024 * 1024


def _params(sem, vmem=VMEM_LIMIT):
    return pltpu.CompilerParams(dimension_semantics=sem, vmem_limit_bytes=vmem)


def _silu(t):
    return t * jax.nn.sigmoid(t)


def _dot(a, b):
    return jnp.dot(a, b, preferred_element_type=F32)


def _dot_nt(a, b):
    return lax.dot_general(a, b, (((1,), (1,)), ((), ())), preferred_element_type=F32)


def _ada_kernel(c_ref, w_ref, b_ref, o_ref):
    s = _silu(c_ref[...]).astype(BF16)
    o_ref[0] = _dot(s, w_ref[0].astype(BF16)) + b_ref[0]


def _ada(cc, w_ada, b_ada):
    depth, d, d6 = w_ada.shape
    rows = cc.shape[0]
    tn = 1536 if d6 % 1536 == 0 else d6
    return pl.pallas_call(
        _ada_kernel,
        grid=(depth, d6 // tn),
        in_specs=[
            pl.BlockSpec((rows, d), lambda l, n: (0, 0)),
            pl.BlockSpec((1, d, tn), lambda l, n: (l, 0, n)),
            pl.BlockSpec((1, 1, tn), lambda l, n: (l, 0, n)),
        ],
        out_specs=pl.BlockSpec((1, rows, tn), lambda l, n: (l, 0, n)),
        out_shape=jax.ShapeDtypeStruct((depth, rows, d6), F32),
        compiler_params=_params(("parallel", "parallel")),
        name="ada",
    )(cc, w_ada, b_ada.reshape(depth, 1, d6))


def _block_scan(t, rowm, reverse):
    n = t.shape[0]
    s = 1
    while s < GLA_SUB:
        if reverse:
            t = t + jnp.where(rowm < GLA_SUB - s, pltpu.roll(t, n - s, 0), 0.0)
        else:
            t = t + jnp.where(rowm >= s, pltpu.roll(t, s, 0), 0.0)
        s *= 2
    return t


def _inproj_kernel(*refs, rope):
    if rope:
        (x_ref, mod_ref, g1_ref, w_ref, ones_ref, gq_ref, gk_ref, gw_ref, gb_ref, cos_ref, sin_ref,
         qa_ref, ka_ref, va_ref, hc_ref, qg_ref, kg_ref, vg_ref, sr_ref, cf_ref, tf_ref, cb_ref, tb_ref) = refs
    else:
        (x_ref, mod_ref, g1_ref, w_ref, ones_ref, gq_ref, gk_ref, gw_ref, gb_ref,
         qa_ref, ka_ref, va_ref, hc_ref, qg_ref, kg_ref, vg_ref, sr_ref, cf_ref, tf_ref, cb_ref, tb_ref) = refs
    x = x_ref[0]
    ms = jnp.mean(x * x, axis=-1, keepdims=True)
    h = x * lax.rsqrt(ms + EPS) * g1_ref[...]
    hb = (h * (1.0 + mod_ref[0, 1:2, :]) + mod_ref[0, 0:1, :]).astype(BF16)

    def seg(lo, hi):
        return _dot(hb, w_ref[:, lo:hi])

    def head_norm(t, g_ref):
        ss = _dot((t * t).astype(BF16), ones_ref[...])
        return (t * lax.rsqrt(ss * (1.0 / NA_HEAD_DIM) + EPS) * g_ref[...]).astype(BF16)

    qa_ref[0] = head_norm(seg(_C_QA, _C_KA), gq_ref)
    ka_ref[0] = head_norm(seg(_C_KA, _C_VA), gk_ref)
    va_ref[0] = seg(_C_VA, _C_UB).astype(BF16)

    ub = seg(_C_UB, _C_QG)
    hc_ref[0] = ub[:, :CONV_CH] * jax.nn.sigmoid(ub[:, CONV_CH:])

    qkg = seg(_C_QG, _C_VG)
    qg = qkg[:, :GLA_K]
    kg = qkg[:, GLA_K:]
    if rope:
        cos = cos_ref[...]
        sin = sin_ref[...]
        first = (lax.broadcasted_iota(jnp.int32, (1, GLA_K), 1) & 15) < 8

        def rot(t):
            partner = jnp.where(first, pltpu.roll(t, GLA_K - 8, 1), pltpu.roll(t, 8, 1))
            return t * cos + partner * sin

        qg = rot(qg)
        kg = rot(kg)
    qg_ref[0] = qg * (GLA_DK ** -0.5)
    kg_ref[0] = kg
    vg_ref[0] = seg(_C_VG, _C_RG)
    sr_ref[0] = _silu(seg(_C_RG, _C_Z))

    z = seg(_C_Z, IN_WIDTH_PAD).astype(BF16)
    a = _dot(z, gw_ref[...]) + gb_ref[...]
    la = (jnp.minimum(a, 0.0) - jnp.log(1.0 + jnp.exp(-jnp.abs(a)))) * (LOG2E / GLA_TAU)
    laf = la[:, :GLA_K]
    lab = la[:, GLA_K:]
    rowm = lax.broadcasted_iota(jnp.int32, laf.shape, 0) & (GLA_SUB - 1)
    pf = _block_scan(laf, rowm, False)
    sf = _block_scan(laf, rowm, True)
    pb = _block_scan(lab, rowm, False)
    sb = _block_scan(lab, rowm, True)
    cf_ref[0] = pf
    tf_ref[0] = pf + sf - laf
    cb_ref[0] = sb
    tb_ref[0] = pb + sb - lab


def _inproj(x, mod, lw, rope_tabs, tm):
    b, n, d = x.shape
    nt = n // tm
    rope = rope_tabs is not None
    mod_map = (lambda bi, ti: (bi, 0, 0)) if mod.shape[0] == b else (lambda bi, ti: (0, 0, 0))
    const = lambda bi, ti: (0, 0)
    tile = lambda bi, ti: (bi, ti, 0)
    in_specs = [
        pl.BlockSpec((1, tm, d), tile),
        pl.BlockSpec((1, 6, d), mod_map),
        pl.BlockSpec((1, d), const),
        pl.BlockSpec((d, IN_WIDTH_PAD), const),
        pl.BlockSpec((NA_WIDTH, NA_WIDTH), const),
        pl.BlockSpec((1, NA_WIDTH), const),
        pl.BlockSpec((1, NA_WIDTH), const),
        pl.BlockSpec((LANES, 2 * GLA_K), const),
        pl.BlockSpec((1, 2 * GLA_K), const),
    ]
    args = [x, mod, lw["g1"], lw["w_in"], lw["ones_na"], lw["gq"], lw["gk"], lw["gw"], lw["gb"]]
    if rope:
        in_specs += [pl.BlockSpec((tm, GLA_K), lambda bi, ti: (ti, 0))] * 2
        args += list(rope_tabs)
    widths = [(NA_WIDTH, BF16)] * 3 + [(CONV_CH, F32), (GLA_K, F32), (GLA_K, F32), (GLA_V, F32), (GLA_V, F32)] + [
        (GLA_K, F32)] * 4
    out_specs = [pl.BlockSpec((1, tm, w), tile) for w, _ in widths]
    out_shape = [jax.ShapeDtypeStruct((b, n, w), dt) for w, dt in widths]
    return pl.pallas_call(
        functools.partial(_inproj_kernel, rope=rope),
        grid=(b, nt),
        in_specs=in_specs,
        out_specs=out_specs,
        out_shape=out_shape,
        compiler_params=_params(("parallel", "parallel")),
        name="inproj_rope" if rope else "inproj",
    )(*args)


def _softmax_pv(s_list, v_list):
    m = s_list[0].max(axis=-1, keepdims=True)
    for s in s_list[1:]:
        m = jnp.maximum(m, s.max(axis=-1, keepdims=True))
    l = 0.0
    o = 0.0
    for s, v in zip(s_list, v_list):
        p = jnp.exp2(s - m)
        l = l + p.sum(axis=-1, keepdims=True)
        o = o + _dot(p.astype(BF16), v)
    return o / l


def _na_kernel(q_ref, k_ref, v_ref, kc_ref, vc_ref, bias_ref, o_ref, sl_ref, sc_ref, *, n_rows):
    j = pl.program_id(1)
    w0 = jnp.clip(NA_TILE_ROWS * j - NA_WIN_R // 2, 0, n_rows - NA_KEY_ROWS)
    start = pl.multiple_of(w0 * GRID_W, GRID_W)
    keys = pl.ds(start, NA_KEY_ROWS * GRID_W)
    low = lax.broadcasted_iota(jnp.int32, (1, LANES), 1) < NA_HEAD_DIM
    one = jnp.ones((), BF16)

    def scores(h):
        lanes = slice((h // 2) * LANES, (h // 2 + 1) * LANES)
        qp = q_ref[0, :, lanes]
        qe = jnp.where(low if h % 2 == 0 else jnp.logical_not(low), qp, jnp.zeros_like(qp))
        sl_ref[h % 2] = _dot_nt(qe, k_ref[0, keys, lanes]) + bias_ref[0, h]
        sc_ref[h % 2] = _dot_nt(qe, kc_ref[0, :, lanes])

    def attend(h):
        lanes = slice((h // 2) * LANES, (h // 2 + 1) * LANES)
        mine = low if h % 2 == 0 else jnp.logical_not(low)
        s_loc = sl_ref[h % 2]
        s_ctx = sc_ref[h % 2]
        m = jnp.maximum(s_loc.max(axis=-1, keepdims=True), s_ctx.max(axis=-1, keepdims=True))
        v_loc = jnp.where(mine, v_ref[0, keys, lanes], one)
        v_ctx = jnp.where(mine, vc_ref[0, :, lanes], one)
        full = (_dot(jnp.exp2(s_loc - m).astype(BF16), v_loc)
                + _dot(jnp.exp2(s_ctx - m).astype(BF16), v_ctx))
        return full / pltpu.roll(full, NA_HEAD_DIM, 1)

    scores(0)
    for p in range(NA_WIDTH // LANES):
        outs = []
        for e in range(2):
            h = 2 * p + e
            if h + 1 < NA_HEADS:
                scores(h + 1)
            outs.append(attend(h))
        o_ref[0, :, p * LANES:(p + 1) * LANES] = jnp.where(low, outs[0], outs[1]).astype(BF16)


def _na_bias_table(rpb, n_rows):
    n_tiles = n_rows // NA_TILE_ROWS
    cq = np.arange(GRID_W)
    cs = np.clip(cq - NA_WIN_C // 2, 0, GRID_W - NA_WIN_C)
    ck = np.arange(GRID_W)[None, :]
    col_valid = (ck >= cs[:, None]) & (ck < cs[:, None] + NA_WIN_C)
    padded = jnp.pad(rpb.astype(F32), ((0, 0), (0, 0), (GRID_W, GRID_W)))
    rows_q = [padded[:, :, GRID_W + NA_WIN_C - 1 - q:GRID_W + NA_WIN_C - 1 - q + GRID_W] for q in range(GRID_W)]
    blocks = jnp.stack(rows_q, axis=2)
    blocks = jnp.where(jnp.asarray(col_valid)[None, None], blocks * LOG2E, MASK_NEG)
    masked_block = jnp.full((NA_HEADS, GRID_W, GRID_W), MASK_NEG, F32)
    pats = []
    for jt in (0, 1, n_tiles - 1):
        rq0 = jt * NA_TILE_ROWS
        w0 = int(np.clip(rq0 - NA_WIN_R // 2, 0, n_rows - NA_KEY_ROWS))
        q_rows = []
        for ri in range(NA_TILE_ROWS):
            r = rq0 + ri
            r0 = int(np.clip(r - NA_WIN_R // 2, 0, n_rows - NA_WIN_R))
            k_blocks = []
            for rki in range(NA_KEY_ROWS):
                rk = w0 + rki
                k_blocks.append(blocks[:, rk - r + NA_WIN_R - 1] if r0 <= rk < r0 + NA_WIN_R else masked_block)
            q_rows.append(jnp.concatenate(k_blocks, axis=-1))
        pats.append(jnp.concatenate(q_rows, axis=-2))
    return jnp.stack(pats)


def _na_attn(qa, ka, va, kc, vc, bias_tab, n_rows):
    b, n, _ = qa.shape
    lc = kc.shape[1]
    n_tiles = n_rows // NA_TILE_ROWS
    tq = NA_TILE_ROWS * GRID_W
    nk = NA_KEY_ROWS * GRID_W

    def bias_map(bi, j):
        return (jnp.where(j == 0, 0, jnp.where(j == n_tiles - 1, 2, 1)), 0, 0, 0)

    return pl.pallas_call(
        functools.partial(_na_kernel, n_rows=n_rows),
        grid=(b, n_tiles),
        in_specs=[
            pl.BlockSpec((1, tq, NA_WIDTH), lambda bi, j: (bi, j, 0)),
            pl.BlockSpec((1, n, NA_WIDTH), lambda bi, j: (bi, 0, 0)),
            pl.BlockSpec((1, n, NA_WIDTH), lambda bi, j: (bi, 0, 0)),
            pl.BlockSpec((1, lc, NA_WIDTH), lambda bi, j: (bi, 0, 0)),
            pl.BlockSpec((1, lc, NA_WIDTH), lambda bi, j: (bi, 0, 0)),
            pl.BlockSpec((1, NA_HEADS, tq, nk), bias_map),
        ],
        out_specs=pl.BlockSpec((1, tq, NA_WIDTH), lambda bi, j: (bi, j, 0)),
        out_shape=jax.ShapeDtypeStruct((b, n, NA_WIDTH), BF16),
        scratch_shapes=[pltpu.VMEM((2, tq, nk), F32), pltpu.VMEM((2, tq, lc), F32)],
        compiler_params=_params(("parallel", "arbitrary")),
        name="na_attn",
    )(qa, ka, va, kc, vc, bias_tab)


def _ctx_attn_kernel(q_ref, k_ref, v_ref, o_ref):
    low = lax.broadcasted_iota(jnp.int32, (1, LANES), 1) < NA_HEAD_DIM
    for p in range(NA_WIDTH // LANES):
        lanes = slice(p * LANES, (p + 1) * LANES)
        qp = q_ref[0, :, lanes]
        kp = k_ref[0, :, lanes]
        vp = v_ref[0, :, lanes]
        outs = []
        for e in range(2):
            qe = jnp.where(low if e == 0 else jnp.logical_not(low), qp, jnp.zeros_like(qp))
            outs.append(_softmax_pv([_dot_nt(qe, kp)], [vp]))
        o_ref[0, :, lanes] = jnp.where(low, outs[0], outs[1]).astype(BF16)


def _ctx_attn(q, k, v):
    b, lc, _ = q.shape
    spec = pl.BlockSpec((1, lc, NA_WIDTH), lambda bi: (bi, 0, 0))
    return pl.pallas_call(
        _ctx_attn_kernel,
        grid=(b,),
        in_specs=[spec, spec, spec],
        out_specs=spec,
        out_shape=jax.ShapeDtypeStruct((b, lc, NA_WIDTH), BF16),
        compiler_params=_params(("parallel",)),
        name="ctx_attn",
    )(q, k, v)


CONV_HALO = 16


CONV_TILE = 64


def _conv_window(h_ref, t0, n):
    t0 = pl.multiple_of(t0, CONV_TILE)
    before = h_ref[0, pl.ds(pl.multiple_of(jnp.maximum(t0 - CONV_HALO, 0), CONV_HALO), CONV_HALO), :]
    after = h_ref[0, pl.ds(pl.multiple_of(jnp.minimum(t0 + CONV_TILE, n - CONV_HALO), CONV_HALO), CONV_HALO), :]
    before = jnp.where(t0 > 0, before, 0.0)
    after = jnp.where(t0 + CONV_TILE < n, after, 0.0)
    return jnp.concatenate([before, h_ref[0, pl.ds(t0, CONV_TILE), :], after], axis=0)


def _conv_stages(h_ref, t0, tm, w_ref, cb_ref, lg_ref, lb_ref, pw_ref, pb_ref, out_ref, slot):
    shift = CONV_HALO - CONV_WIDTH // 2
    rows = CONV_TILE + 2 * CONV_HALO
    n = h_ref.shape[1]
    stages = []
    for s in range(tm // CONV_TILE):
        state = {}

        def taps(phases, s=s, state=state):
            chunk = _conv_window(h_ref, t0 + s * CONV_TILE, n)
            acc = state.get("acc", jnp.zeros((CONV_TILE, CONV_CH), F32))
            for phase in phases:
                off = (phase + shift) % 8
                base = phase + shift - off
                shifted = pltpu.roll(chunk, rows - off, 0) if off else chunk
                shifted = shifted[base:base + CONV_TILE + 8 * ((CONV_WIDTH - 1) // 8), :]
                for k in range(phase, CONV_WIDTH, 8):
                    acc = acc + shifted[k - phase:k - phase + CONV_TILE, :] * w_ref[k:k + 1, :]
            state["acc"] = acc

        def finish(s=s, state=state):
            hcv = state["acc"] + cb_ref[...]
            mu = jnp.mean(hcv, axis=-1, keepdims=True)
            ctr = hcv - mu
            var = jnp.mean(ctr * ctr, axis=-1, keepdims=True)
            y = _silu(ctr * lax.rsqrt(var + EPS) * lg_ref[...] + lb_ref[...])
            out = (_dot(y.astype(BF16), pw_ref[...]) + pb_ref[...]).astype(BF16)
            out_ref[slot, s * CONV_TILE:(s + 1) * CONV_TILE, :] = out

        stages.append(functools.partial(taps, range(0, 4)))
        stages.append(lambda taps=taps, finish=finish: (taps(range(4, 8)), finish()))
    return stages


GLA_INTRA_ROWS = 64


def _gla_intra(q_ref, k_ref, v_ref, c_ref, exp_ref, o_ref, r0, rowm, forward):
    sub = GLA_INTRA_ROWS
    q = q_ref[0, pl.ds(r0, sub), :]
    k = k_ref[0, pl.ds(r0, sub), :]
    v = v_ref[0, pl.ds(r0, sub), :]
    cum = c_ref[0, pl.ds(r0, sub), :]
    acc = _dot((q * k).astype(BF16), exp_ref[...]) * v

    def shift(t, d):
        return pltpu.roll(t, d if forward else sub - d, 0)

    for b in range(8):
        kb, cb, vb = (shift(k, b), shift(cum, b), shift(v, b)) if b else (k, cum, v)
        for d in (b, 8 + b):
            if d == 0:
                continue
            kd, cd, vd = (shift(kb, 8), shift(cb, 8), shift(vb, 8)) if d >= 8 else (kb, cb, vb)
            valid = (rowm >= d) if forward else (rowm < GLA_SUB - d)
            pd = q * kd * jnp.exp2(jnp.where(valid, cum - cd, MASK_NEG))
            acc = acc + _dot(pd.astype(BF16), exp_ref[...]) * vd
    o_ref[0, pl.ds(r0, sub), :] = acc


def _gla_prepare(q_ref, k_ref, v_ref, c_ref, t_ref, qt_ref, kh_ref, vt_ref):
    cum = c_ref[0]
    qt_ref[...] = (q_ref[0] * jnp.exp2(cum)).astype(BF16)
    kh_ref[...] = (k_ref[0] * jnp.exp2(t_ref[0] - cum)).astype(BF16)
    vt_ref[...] = v_ref[0].T.astype(BF16)


def _gla_updates(ib, vt_ref, kh_ref, u_ref, chunk_in_group):
    per_pass = GLA_INTRA_ROWS // GLA_SUB
    g0 = pl.multiple_of(((ib * GLA_INTRA_ROWS) // LANES) * LANES, LANES)
    first = (ib * per_pass) % (LANES // GLA_SUB)
    vtg = vt_ref[:, pl.ds(g0, LANES)]
    khg = kh_ref[pl.ds(g0, LANES), :]
    for c in range(per_pass):
        u_ref[ib * per_pass + c] = _dot(jnp.where(chunk_in_group == first + c, vtg, jnp.zeros((), BF16)), khg)


def _gla_recur(i, t_ref, s_ref, u_ref, sbf_ref, same_head):
    state = s_ref[...]
    sbf_ref[i] = jnp.where(same_head, state, 0.0).astype(BF16)
    r0 = pl.multiple_of(i * GLA_SUB, GLA_SUB)
    s_ref[...] = jnp.exp2(t_ref[0, pl.ds(r0, 1), :]) * state + u_ref[i]


def _gla_kernel(qf_ref, kf_ref, vf_ref, cf_ref, tf_ref, qb_ref, kb_ref, vb_ref, cb_ref, tb_ref, s0f_ref, s0b_ref,
                exp_ref, of_ref, ob_ref, sfo_ref, sbo_ref, sf_ref, sb_ref, qtf_ref, khf_ref, vtf_ref, uf_ref,
                sbff_ref, qtb_ref, khb_ref, vtb_ref, ub_ref, sbfb_ref):
    j = pl.program_id(1)
    tn = qf_ref.shape[1]
    n_chunks = tn // GLA_SUB

    @pl.when(j == 0)
    def _():
        sf_ref[...] = s0f_ref[0]
        sb_ref[...] = s0b_ref[0]

    _gla_prepare(qf_ref, kf_ref, vf_ref, cf_ref, tf_ref, qtf_ref, khf_ref, vtf_ref)
    _gla_prepare(qb_ref, kb_ref, vb_ref, cb_ref, tb_ref, qtb_ref, khb_ref, vtb_ref)

    rowm = lax.broadcasted_iota(jnp.int32, (GLA_INTRA_ROWS, GLA_K), 0) & (GLA_SUB - 1)
    chunk_in_group = lax.broadcasted_iota(jnp.int32, (1, LANES), 1) // GLA_SUB

    def intra(ib, carry):
        r0 = pl.multiple_of(ib * GLA_INTRA_ROWS, GLA_INTRA_ROWS)
        _gla_intra(qf_ref, kf_ref, vf_ref, cf_ref, exp_ref, of_ref, r0, rowm, True)
        _gla_intra(qb_ref, kb_ref, vb_ref, cb_ref, exp_ref, ob_ref, r0, rowm, False)
        _gla_updates(ib, vtf_ref, khf_ref, uf_ref, chunk_in_group)
        _gla_updates(ib, vtb_ref, khb_ref, ub_ref, chunk_in_group)
        return carry

    lax.fori_loop(0, tn // GLA_INTRA_ROWS, intra, 0)

    head_of_row = lax.broadcasted_iota(jnp.int32, (GLA_V, GLA_K), 0) // GLA_DV
    head_of_col = lax.broadcasted_iota(jnp.int32, (GLA_V, GLA_K), 1) // GLA_DK
    same_head = head_of_row == head_of_col

    def recur(i, carry):
        _gla_recur(i, tf_ref, sf_ref, uf_ref, sbff_ref, same_head)
        _gla_recur(n_chunks - 1 - i, tb_ref, sb_ref, ub_ref, sbfb_ref, same_head)
        return carry

    lax.fori_loop(0, n_chunks, recur, 0)

    for i in range(n_chunks):
        rows = slice(i * GLA_SUB, (i + 1) * GLA_SUB)
        of_ref[0, rows, :] = of_ref[0, rows, :] + _dot_nt(qtf_ref[rows, :], sbff_ref[i])
        ob_ref[0, rows, :] = ob_ref[0, rows, :] + _dot_nt(qtb_ref[rows, :], sbfb_ref[i])

    @pl.when(j == pl.num_programs(1) - 1)
    def _():
        sfo_ref[0] = sf_ref[...]
        sbo_ref[0] = sb_ref[...]


def _gla(qg, kg, vg, cf, tf, cb, tb, s0f, s0b, expand, tn):
    b, n, _ = qg.shape
    nt = n // tn
    fwd = lambda bi, j: (bi, j, 0)
    bwd = lambda bi, j: (bi, nt - 1 - j, 0)
    per_b = lambda bi, j: (bi, 0, 0)

    def spec(w, m):
        return pl.BlockSpec((1, tn, w), m)

    state_spec = pl.BlockSpec((1, GLA_V, GLA_K), per_b)
    return pl.pallas_call(
        _gla_kernel,
        grid=(b, nt),
        in_specs=[spec(GLA_K, fwd), spec(GLA_K, fwd), spec(GLA_V, fwd), spec(GLA_K, fwd), spec(GLA_K, fwd),
                  spec(GLA_K, bwd), spec(GLA_K, bwd), spec(GLA_V, bwd), spec(GLA_K, bwd), spec(GLA_K, bwd),
                  state_spec, state_spec,
                  pl.BlockSpec((GLA_K, GLA_V), lambda bi, j: (0, 0))],
        out_specs=[spec(GLA_V, fwd), spec(GLA_V, bwd), state_spec, state_spec],
        out_shape=[jax.ShapeDtypeStruct((b, n, GLA_V), F32), jax.ShapeDtypeStruct((b, n, GLA_V), F32),
                   jax.ShapeDtypeStruct((b, GLA_V, GLA_K), F32), jax.ShapeDtypeStruct((b, GLA_V, GLA_K), F32)],
        scratch_shapes=[pltpu.VMEM((GLA_V, GLA_K), F32), pltpu.VMEM((GLA_V, GLA_K), F32)] + 2 * [
            pltpu.VMEM((tn, GLA_K), BF16), pltpu.VMEM((tn, GLA_K), BF16), pltpu.VMEM((GLA_V, tn), BF16),
            pltpu.VMEM((tn // GLA_SUB, GLA_V, GLA_K), F32), pltpu.VMEM((tn // GLA_SUB, GLA_V, GLA_K), BF16)],
        compiler_params=_params(("parallel", "arbitrary")),
        name="gla",
    )(qg, kg, vg, cf, tf, qg, kg, vg, cb, tb, s0f, s0b, expand)


def _outmlp_kernel(x_ref, oa_ref, hc_ref, of_ref, obw_ref, sr_ref, mod_ref, gg_ref, ones_ref, wo_ref, g2_ref,
                   w1_ref, w2_ref, cw_ref, cb_ref, lg_ref, lb_ref, pw_ref, pb_ref, o_ref, ob_ref, *, tf):
    x = x_ref[0]
    tm = x.shape[0]
    ti = pl.program_id(1)

    def conv_stages(i, slot):
        return _conv_stages(hc_ref, i * tm, tm, cw_ref, cb_ref, lg_ref, lb_ref, pw_ref, pb_ref, ob_ref, slot)

    @pl.when(ti == 0)
    def _():
        for stage in conv_stages(ti, 0):
            stage()

    next_stages = conv_stages(jnp.minimum(ti + 1, pl.num_programs(1) - 1), (ti + 1) % 2)
    ob = ob_ref[ti % 2]
    og = of_ref[0] + obw_ref[0]
    ss = _dot((og * og).astype(BF16), ones_ref[...])
    oc = (og * lax.rsqrt(ss * (1.0 / GLA_DV) + EPS) * gg_ref[...]) * sr_ref[0]
    c_b = NA_WIDTH
    c_c = NA_WIDTH + CONV_CH
    y = _dot(oa_ref[0], wo_ref[0:c_b, :]) + _dot(ob, wo_ref[c_b:c_c, :]) + _dot(oc.astype(BF16), wo_ref[c_c:, :])
    x1 = x + mod_ref[0, 2:3, :] * y
    ms = jnp.mean(x1 * x1, axis=-1, keepdims=True)
    h2 = x1 * lax.rsqrt(ms + EPS) * g2_ref[...]
    h2 = (h2 * (1.0 + mod_ref[0, 4:5, :]) + mod_ref[0, 3:4, :]).astype(BF16)
    acc = jnp.zeros(x.shape, F32)
    n_ff = w1_ref.shape[1] // tf
    for jf in range(n_ff):
        a = jnp.maximum(_dot(h2, w1_ref[:, jf * tf:(jf + 1) * tf]), 0.0)
        acc = acc + _dot((a * a).astype(BF16), w2_ref[jf * tf:(jf + 1) * tf, :])
        for stage in next_stages[jf * len(next_stages) // n_ff:(jf + 1) * len(next_stages) // n_ff]:
            stage()
    o_ref[0] = x1 + mod_ref[0, 5:6, :] * acc


def _outmlp(x, oa, hc, of, obw, sr, mod, lw, tm):
    b, n, d = x.shape
    assert tm % CONV_TILE == 0 and n >= CONV_HALO
    vec = pl.BlockSpec((1, CONV_CH), lambda bi, ti: (0, 0))
    dff = lw["w1"].shape[1]
    tf = min(dff, 1024)
    mod_map = (lambda bi, ti: (bi, 0, 0)) if mod.shape[0] == b else (lambda bi, ti: (0, 0, 0))
    const = lambda bi, ti: (0, 0)
    tile = lambda bi, ti: (bi, ti, 0)
    once = pl.Buffered(1)
    return pl.pallas_call(
        functools.partial(_outmlp_kernel, tf=tf),
        grid=(b, n // tm),
        in_specs=[
            pl.BlockSpec((1, tm, d), tile),
            pl.BlockSpec((1, tm, NA_WIDTH), tile),
            pl.BlockSpec((1, n, CONV_CH), lambda bi, ti: (bi, 0, 0)),
            pl.BlockSpec((1, tm, GLA_V), tile),
            pl.BlockSpec((1, tm, GLA_V), tile),
            pl.BlockSpec((1, tm, GLA_V), tile),
            pl.BlockSpec((1, 6, d), mod_map),
            pl.BlockSpec((1, GLA_V), const),
            pl.BlockSpec((GLA_V, GLA_V), const),
            pl.BlockSpec((NA_WIDTH + CONV_CH + GLA_V, d), const, pipeline_mode=once),
            pl.BlockSpec((1, d), const),
            pl.BlockSpec((d, dff), const, pipeline_mode=once),
            pl.BlockSpec((dff, d), const, pipeline_mode=once),
            pl.BlockSpec((CONV_WIDTH, CONV_CH), const),
            vec, vec, vec,
            pl.BlockSpec((CONV_CH, CONV_CH), const),
            vec,
        ],
        out_specs=pl.BlockSpec((1, tm, d), tile),
        out_shape=jax.ShapeDtypeStruct((b, n, d), F32),
        scratch_shapes=[pltpu.VMEM((2, tm, CONV_CH), BF16)],
        compiler_params=_params(("parallel", "arbitrary")),
        name="outmlp",
    )(x, oa, hc, of, obw, sr, mod, lw["gg"], lw["ones_gla"], lw["w_out"], lw["g2"], lw["w1"], lw["w2"],
      lw["conv_w"], lw["conv_b"], lw["conv_ln_g"], lw["conv_ln_b"], lw["conv_pw_w"], lw["conv_pw_b"])


def _block_ones(n, group):
    idx = np.arange(n) // group
    return jnp.asarray(idx[:, None] == idx[None, :], BF16)


def _rope_tables(n):
    half = GLA_DK // 2
    n_freq = half // 2
    lane = np.arange(GLA_K) % GLA_DK
    use_col = lane >= half
    freq_idx = lane % n_freq
    second = (lane % half) >= n_freq
    pos = jnp.arange(n)
    inv_freq = ROPE_THETA ** (-jnp.arange(n_freq, dtype=F32) / n_freq)
    p = jnp.where(jnp.asarray(use_col)[None, :], (pos % GRID_W)[:, None], (pos // GRID_W)[:, None]).astype(F32)
    ang = p * inv_freq[jnp.asarray(freq_idx)][None, :]
    sign = jnp.where(jnp.asarray(second), 1.0, -1.0).astype(F32)
    return jnp.cos(ang), jnp.sin(ang) * sign[None, :]


def _layer_weights(l, w):
    d = w["w_in"].shape[1]
    gw = jnp.zeros((LANES, 2 * GLA_K), F32)
    gw = gw.at[:GLA_GATE_RANK, :GLA_K].set(w["gla_gw_f"][l])
    gw = gw.at[GLA_GATE_RANK:2 * GLA_GATE_RANK, GLA_K:].set(w["gla_gw_b"][l])
    row = lambda t: t.reshape(1, -1).astype(F32)
    return {
        "g1": row(w["norm1_g"][l]),
        "w_in": jnp.pad(w["w_in"][l], ((0, 0), (0, IN_WIDTH_PAD - IN_WIDTH))).astype(BF16),
        "ones_na": _block_ones(NA_WIDTH, NA_HEAD_DIM),
        "gq": row(jnp.tile(w["na_q_g"][l], NA_HEADS) * (NA_HEAD_DIM ** -0.5 * LOG2E)),
        "gk": row(jnp.tile(w["na_k_g"][l], NA_HEADS)),
        "gw": gw.astype(BF16),
        "gb": row(jnp.concatenate([w["gla_gb_f"][l], w["gla_gb_b"][l]])),
        "conv_w": w["conv_w"][l].astype(F32),
        "conv_b": row(w["conv_b"][l]),
        "conv_ln_g": row(w["conv_ln_g"][l]),
        "conv_ln_b": row(w["conv_ln_b"][l]),
        "conv_pw_w": w["conv_pw_w"][l].astype(BF16),
        "conv_pw_b": row(w["conv_pw_b"][l]),
        "gg": row(jnp.tile(w["gla_out_g"][l], GLA_HEADS)),
        "ones_gla": _block_ones(GLA_V, GLA_DV),
        "w_out": w["w_out"][l].astype(BF16),
        "g2": row(w["norm2_g"][l]),
        "w1": w["w_mlp_in"][l].astype(BF16),
        "w2": w["w_mlp_out"][l].astype(BF16),
    }


def kernel(x, c, ctx, c_ctx, w_ada, b_ada, norm1_g, w_in, na_q_g, na_k_g, na_rpb, conv_w, conv_b, conv_ln_g,
           conv_ln_b, conv_pw_w, conv_pw_b, gla_gw_f, gla_gb_f, gla_gw_b, gla_gb_b, gla_out_g, w_out, norm2_g,
           w_mlp_in, w_mlp_out):
    w = dict(norm1_g=norm1_g, w_in=w_in, na_q_g=na_q_g, na_k_g=na_k_g, conv_w=conv_w, conv_b=conv_b,
             conv_ln_g=conv_ln_g, conv_ln_b=conv_ln_b, conv_pw_w=conv_pw_w, conv_pw_b=conv_pw_b,
             gla_gw_f=gla_gw_f, gla_gb_f=gla_gb_f, gla_gw_b=gla_gw_b, gla_gb_b=gla_gb_b, gla_out_g=gla_out_g,
             w_out=w_out, norm2_g=norm2_g, w_mlp_in=w_mlp_in, w_mlp_out=w_mlp_out)
    b, n, d = x.shape
    lc = ctx.shape[1]
    depth = w_ada.shape[0]
    n_rows = n // GRID_W
    assert n % GRID_W == 0 and n_rows % NA_TILE_ROWS == 0 and n_rows >= NA_KEY_ROWS + 1
    tm_x = tm_c = tn_x = tn_c = 256
    tm_in = 512 if n % 512 == 0 else tm_x
    assert n % tm_x == 0 and lc % tm_c == 0

    ada_rows = -(-(b + 1) // 8) * 8
    cc = jnp.concatenate([c, c_ctx[None, :], jnp.zeros((ada_rows - b - 1, d), c.dtype)], axis=0).astype(F32)
    mod_all = _ada(cc, w_ada, b_ada)

    rope_tabs = _rope_tables(n)
    expand = jnp.asarray((np.arange(GLA_K) // GLA_DK)[:, None] == (np.arange(GLA_V) // GLA_DV)[None, :], BF16)
    zero_state = jnp.zeros((b, GLA_V, GLA_K), F32)

    xc = ctx
    for l in range(depth):
        lw = _layer_weights(l, w)
        mod = mod_all[l].reshape(ada_rows, 6, d)
        mod_x = mod[:b]
        mod_c = mod[b:b + 1]
        need_ctx_out = l < depth - 1

        (qac, kac, vac, hcc, qgc, kgc, vgc, src, cfc, tfc, cbc, tbc) = _inproj(xc, mod_c, lw, None, tm_c)
        (qa, ka, va, hc, qg, kg, vg, sr, cf, tf, cb, tb) = _inproj(x, mod_x, lw, rope_tabs, tm_in)

        ofc, obc, s_f, s_b = _gla(qgc, kgc, vgc, cfc, tfc, cbc, tbc, zero_state, zero_state, expand, tn_c)
        of, obw, _, _ = _gla(qg, kg, vg, cf, tf, cb, tb, s_f, s_b, expand, tn_x)

        bias_tab = _na_bias_table(na_rpb[l], n_rows)
        oa = _na_attn(qa, ka, va, kac, vac, bias_tab, n_rows)
        x = _outmlp(x, oa, hc, of, obw, sr, mod_x, lw, tm_in)

        if need_ctx_out:
            oac = _ctx_attn(qac, kac, vac)
            xc = _outmlp(xc, oac, hcc, ofc, obc, src, mod_c, lw, tm_c)
    return x
```

```python
import functools

import numpy as np
import jax
import jax.numpy as jnp
from jax import lax
from jax.experimental import pallas as pl
from jax.experimental.pallas import tpu as pltpu

F32 = jnp.float32
BF16 = jnp.bfloat16

GRID_W = 64
EPS = 1e-6

NA_HEADS = 8
NA_HEAD_DIM = 64
NA_WIN_R = 8
NA_WIN_C = 16
NA_WIDTH = NA_HEADS * NA_HEAD_DIM

CONV_CH = 256
CONV_WIDTH = 31

GLA_HEADS = 4
GLA_DK = 32
GLA_DV = 64
GLA_GATE_RANK = 16
GLA_TAU = 16.0
GLA_K = GLA_HEADS * GLA_DK
GLA_V = GLA_HEADS * GLA_DV
ROPE_THETA = 10000.0

_C_QA, _C_KA, _C_VA, _C_UB = 0, NA_WIDTH, 2 * NA_WIDTH, 3 * NA_WIDTH
_C_QG = _C_UB + 2 * CONV_CH
_C_KG = _C_QG + GLA_K
_C_VG = _C_KG + GLA_K
_C_RG = _C_VG + GLA_V
_C_Z = _C_RG + GLA_V
IN_WIDTH = _C_Z + 2 * GLA_GATE_RANK
LANES = 128
IN_WIDTH_PAD = -(-IN_WIDTH // LANES) * LANES

GLA_SUB = 16
NA_TILE_ROWS = 4
NA_KEY_ROWS = NA_TILE_ROWS + NA_WIN_R - 1
MASK_NEG = -1e30
LOG2E = 1.4426950408889634
VMEM_LIMIT = 56 * 1024 * 1024


def _params(sem, vmem=VMEM_LIMIT):
    return pltpu.CompilerParams(dimension_semantics=sem, vmem_limit_bytes=vmem)


def _silu(t):
    return t * jax.nn.sigmoid(t)


def _dot(a, b):
    return jnp.dot(a, b, preferred_element_type=F32)


def _dot_nt(a, b):
    return lax.dot_general(a, b, (((1,), (1,)), ((), ())), preferred_element_type=F32)


def _ada_kernel(c_ref, w_ref, b_ref, o_ref):
    s = _silu(c_ref[...]).astype(BF16)
    o_ref[0] = _dot(s, w_ref[0].astype(BF16)) + b_ref[0]


def _ada(cc, w_ada, b_ada):
    depth, d, d6 = w_ada.shape
    rows = cc.shape[0]
    tn = 1536 if d6 % 1536 == 0 else d6
    return pl.pallas_call(
        _ada_kernel,
        grid=(depth, d6 // tn),
        in_specs=[
            pl.BlockSpec((rows, d), lambda l, n: (0, 0)),
            pl.BlockSpec((1, d, tn), lambda l, n: (l, 0, n)),
            pl.BlockSpec((1, 1, tn), lambda l, n: (l, 0, n)),
        ],
        out_specs=pl.BlockSpec((1, rows, tn), lambda l, n: (l, 0, n)),
        out_shape=jax.ShapeDtypeStruct((depth, rows, d6), F32),
        compiler_params=_params(("parallel", "parallel")),
        name="ada",
    )(cc, w_ada, b_ada.reshape(depth, 1, d6))


def _block_scan(t, rowm, reverse):
    n = t.shape[0]
    s = 1
    while s < GLA_SUB:
        if reverse:
            t = t + jnp.where(rowm < GLA_SUB - s, pltpu.roll(t, n - s, 0), 0.0)
        else:
            t = t + jnp.where(rowm >= s, pltpu.roll(t, s, 0), 0.0)
        s *= 2
    return t


def _inproj_kernel(*refs, rope):
    if rope:
        (x_ref, mod_ref, g1_ref, w_ref, ones_ref, gq_ref, gk_ref, gw_ref, gb_ref, cos_ref, sin_ref,
         qa_ref, ka_ref, va_ref, hc_ref, qg_ref, kg_ref, vg_ref, sr_ref, cf_ref, tf_ref, cb_ref, tb_ref) = refs
    else:
        (x_ref, mod_ref, g1_ref, w_ref, ones_ref, gq_ref, gk_ref, gw_ref, gb_ref,
         qa_ref, ka_ref, va_ref, hc_ref, qg_ref, kg_ref, vg_ref, sr_ref, cf_ref, tf_ref, cb_ref, tb_ref) = refs
    x = x_ref[0]
    ms = jnp.mean(x * x, axis=-1, keepdims=True)
    h = x * lax.rsqrt(ms + EPS) * g1_ref[...]
    hb = (h * (1.0 + mod_ref[0, 1:2, :]) + mod_ref[0, 0:1, :]).astype(BF16)

    def seg(lo, hi):
        return _dot(hb, w_ref[:, lo:hi])

    def head_norm(t, g_ref):
        ss = _dot((t * t).astype(BF16), ones_ref[...])
        return (t * lax.rsqrt(ss * (1.0 / NA_HEAD_DIM) + EPS) * g_ref[...]).astype(BF16)

    qa_ref[0] = head_norm(seg(_C_QA, _C_KA), gq_ref)
    ka_ref[0] = head_norm(seg(_C_KA, _C_VA), gk_ref)
    va_ref[0] = seg(_C_VA, _C_UB).astype(BF16)

    ub = seg(_C_UB, _C_QG)
    hc_ref[0] = ub[:, :CONV_CH] * jax.nn.sigmoid(ub[:, CONV_CH:])

    qkg = seg(_C_QG, _C_VG)
    qg = qkg[:, :GLA_K]
    kg = qkg[:, GLA_K:]
    if rope:
        cos = cos_ref[...]
        sin = sin_ref[...]
        first = (lax.broadcasted_iota(jnp.int32, (1, GLA_K), 1) & 15) < 8

        def rot(t):
            partner = jnp.where(first, pltpu.roll(t, GLA_K - 8, 1), pltpu.roll(t, 8, 1))
            return t * cos + partner * sin

        qg = rot(qg)
        kg = rot(kg)
    qg_ref[0] = qg * (GLA_DK ** -0.5)
    kg_ref[0] = kg
    vg_ref[0] = seg(_C_VG, _C_RG)
    sr_ref[0] = _silu(seg(_C_RG, _C_Z))

    z = seg(_C_Z, IN_WIDTH_PAD).astype(BF16)
    a = _dot(z, gw_ref[...]) + gb_ref[...]
    la = (jnp.minimum(a, 0.0) - jnp.log(1.0 + jnp.exp(-jnp.abs(a)))) * (LOG2E / GLA_TAU)
    laf = la[:, :GLA_K]
    lab = la[:, GLA_K:]
    rowm = lax.broadcasted_iota(jnp.int32, laf.shape, 0) & (GLA_SUB - 1)
    pf = _block_scan(laf, rowm, False)
    sf = _block_scan(laf, rowm, True)
    pb = _block_scan(lab, rowm, False)
    sb = _block_scan(lab, rowm, True)
    cf_ref[0] = pf
    tf_ref[0] = pf + sf - laf
    cb_ref[0] = sb
    tb_ref[0] = pb + sb - lab


def _inproj(x, mod, lw, rope_tabs, tm):
    b, n, d = x.shape
    nt = n // tm
    rope = rope_tabs is not None
    mod_map = (lambda bi, ti: (bi, 0, 0)) if mod.shape[0] == b else (lambda bi, ti: (0, 0, 0))
    const = lambda bi, ti: (0, 0)
    tile = lambda bi, ti: (bi, ti, 0)
    in_specs = [
        pl.BlockSpec((1, tm, d), tile),
        pl.BlockSpec((1, 6, d), mod_map),
        pl.BlockSpec((1, d), const),
        pl.BlockSpec((d, IN_WIDTH_PAD), const),
        pl.BlockSpec((NA_WIDTH, NA_WIDTH), const),
        pl.BlockSpec((1, NA_WIDTH), const),
        pl.BlockSpec((1, NA_WIDTH), const),
        pl.BlockSpec((LANES, 2 * GLA_K), const),
        pl.BlockSpec((1, 2 * GLA_K), const),
    ]
    args = [x, mod, lw["g1"], lw["w_in"], lw["ones_na"], lw["gq"], lw["gk"], lw["gw"], lw["gb"]]
    if rope:
        in_specs += [pl.BlockSpec((tm, GLA_K), lambda bi, ti: (ti, 0))] * 2
        args += list(rope_tabs)
    widths = [(NA_WIDTH, BF16)] * 3 + [(CONV_CH, F32), (GLA_K, F32), (GLA_K, F32), (GLA_V, F32), (GLA_V, F32)] + [
        (GLA_K, F32)] * 4
    out_specs = [pl.BlockSpec((1, tm, w), tile) for w, _ in widths]
    out_shape = [jax.ShapeDtypeStruct((b, n, w), dt) for w, dt in widths]
    return pl.pallas_call(
        functools.partial(_inproj_kernel, rope=rope),
        grid=(b, nt),
        in_specs=in_specs,
        out_specs=out_specs,
        out_shape=out_shape,
        compiler_params=_params(("parallel", "parallel")),
        name="inproj_rope" if rope else "inproj",
    )(*args)


def _softmax_pv(s_list, v_list):
    m = s_list[0].max(axis=-1, keepdims=True)
    for s in s_list[1:]:
        m = jnp.maximum(m, s.max(axis=-1, keepdims=True))
    l = 0.0
    o = 0.0
    for s, v in zip(s_list, v_list):
        p = jnp.exp2(s - m)
        l = l + p.sum(axis=-1, keepdims=True)
        o = o + _dot(p.astype(BF16), v)
    return o / l


def _na_kernel(q_ref, k_ref, v_ref, kc_ref, vc_ref, bias_ref, o_ref, sl_ref, sc_ref, *, n_rows):
    j = pl.program_id(1)
    w0 = jnp.clip(NA_TILE_ROWS * j - NA_WIN_R // 2, 0, n_rows - NA_KEY_ROWS)
    start = pl.multiple_of(w0 * GRID_W, GRID_W)
    keys = pl.ds(start, NA_KEY_ROWS * GRID_W)
    low = lax.broadcasted_iota(jnp.int32, (1, LANES), 1) < NA_HEAD_DIM
    one = jnp.ones((), BF16)
    pattern = jnp.where(j == 0, 0, jnp.where(j == pl.num_programs(1) - 1, 2, 1))

    def scores(h):
        lanes = slice((h // 2) * LANES, (h // 2 + 1) * LANES)
        qp = q_ref[0, :, lanes]
        qe = jnp.where(low if h % 2 == 0 else jnp.logical_not(low), qp, jnp.zeros_like(qp))
        sl_ref[h % 2] = _dot_nt(qe, k_ref[0, keys, lanes]) + bias_ref[pattern, h]
        sc_ref[h % 2] = _dot_nt(qe, kc_ref[0, :, lanes])

    def attend(h):
        lanes = slice((h // 2) * LANES, (h // 2 + 1) * LANES)
        mine = low if h % 2 == 0 else jnp.logical_not(low)
        s_loc = sl_ref[h % 2]
        s_ctx = sc_ref[h % 2]
        m = jnp.maximum(s_loc.max(axis=-1, keepdims=True), s_ctx.max(axis=-1, keepdims=True))
        v_loc = jnp.where(mine, v_ref[0, keys, lanes], one)
        v_ctx = jnp.where(mine, vc_ref[0, :, lanes], one)
        full = (_dot(jnp.exp2(s_loc - m).astype(BF16), v_loc)
                + _dot(jnp.exp2(s_ctx - m).astype(BF16), v_ctx))
        return full / pltpu.roll(full, NA_HEAD_DIM, 1)

    scores(0)
    for p in range(NA_WIDTH // LANES):
        outs = []
        for e in range(2):
            h = 2 * p + e
            if h + 1 < NA_HEADS:
                scores(h + 1)
            outs.append(attend(h))
        o_ref[0, :, p * LANES:(p + 1) * LANES] = jnp.where(low, outs[0], outs[1]).astype(BF16)


def _na_bias_tables(rpb, n_rows):
    n_tiles = n_rows // NA_TILE_ROWS
    n_layers = rpb.shape[0]
    n_off = 2 * NA_WIN_R - 1
    left = GRID_W - NA_WIN_C - 1
    period = 2 * GRID_W - 1
    padded = jnp.pad(rpb.astype(F32), ((0, 0), (0, 0), (0, 0), (left, period - left - (2 * NA_WIN_C - 1))))
    flat = jnp.tile(padded, (1, 1, 1, GRID_W))[..., :GRID_W * (period - 1)]
    blocks = flat.reshape(n_layers, NA_HEADS, n_off, GRID_W, period - 1)[..., GRID_W - 2:2 * GRID_W - 2]
    cq = np.arange(GRID_W)
    cs = np.clip(cq - NA_WIN_C // 2, 0, GRID_W - NA_WIN_C)
    ck = np.arange(GRID_W)[None, :]
    col_valid = (ck >= cs[:, None]) & (ck < cs[:, None] + NA_WIN_C)
    blocks = jnp.where(jnp.asarray(col_valid), blocks * LOG2E, MASK_NEG)
    masked_block = jnp.full((n_layers, NA_HEADS, 1, GRID_W, GRID_W), MASK_NEG, F32)
    blocks = jnp.concatenate([blocks, masked_block], axis=2)
    which = np.full((3, NA_TILE_ROWS, NA_KEY_ROWS), n_off, np.int32)
    for pat, jt in enumerate((0, 1, n_tiles - 1)):
        rq0 = jt * NA_TILE_ROWS
        w0 = int(np.clip(rq0 - NA_WIN_R // 2, 0, n_rows - NA_KEY_ROWS))
        for ri in range(NA_TILE_ROWS):
            r = rq0 + ri
            r0 = int(np.clip(r - NA_WIN_R // 2, 0, n_rows - NA_WIN_R))
            for rki in range(NA_KEY_ROWS):
                rk = w0 + rki
                if r0 <= rk < r0 + NA_WIN_R:
                    which[pat, ri, rki] = rk - r + NA_WIN_R - 1
    tab = jnp.take(blocks, jnp.asarray(which.reshape(-1)), axis=2)
    tab = tab.reshape(n_layers, NA_HEADS, 3, NA_TILE_ROWS, NA_KEY_ROWS, GRID_W, GRID_W)
    tab = jnp.transpose(tab, (0, 2, 1, 3, 5, 4, 6))
    return tab.reshape(n_layers, 3, NA_HEADS, NA_TILE_ROWS * GRID_W, NA_KEY_ROWS * GRID_W)


def _na_attn(qa, ka, va, kc, vc, bias_tab, n_rows):
    b, n, _ = qa.shape
    lc = kc.shape[1]
    n_tiles = n_rows // NA_TILE_ROWS
    tq = NA_TILE_ROWS * GRID_W
    nk = NA_KEY_ROWS * GRID_W

    return pl.pallas_call(
        functools.partial(_na_kernel, n_rows=n_rows),
        grid=(b, n_tiles),
        in_specs=[
            pl.BlockSpec((1, tq, NA_WIDTH), lambda bi, j: (bi, j, 0)),
            pl.BlockSpec((1, n, NA_WIDTH), lambda bi, j: (bi, 0, 0)),
            pl.BlockSpec((1, n, NA_WIDTH), lambda bi, j: (bi, 0, 0)),
            pl.BlockSpec((1, lc, NA_WIDTH), lambda bi, j: (bi, 0, 0)),
            pl.BlockSpec((1, lc, NA_WIDTH), lambda bi, j: (bi, 0, 0)),
            pl.BlockSpec((3, NA_HEADS, tq, nk), lambda bi, j: (0, 0, 0, 0), pipeline_mode=pl.Buffered(1)),
        ],
        out_specs=pl.BlockSpec((1, tq, NA_WIDTH), lambda bi, j: (bi, j, 0)),
        out_shape=jax.ShapeDtypeStruct((b, n, NA_WIDTH), BF16),
        scratch_shapes=[pltpu.VMEM((2, tq, nk), F32), pltpu.VMEM((2, tq, lc), F32)],
        compiler_params=_params(("parallel", "arbitrary")),
        name="na_attn",
    )(qa, ka, va, kc, vc, bias_tab)


def _ctx_attn_kernel(q_ref, k_ref, v_ref, o_ref):
    low = lax.broadcasted_iota(jnp.int32, (1, LANES), 1) < NA_HEAD_DIM
    for p in range(NA_WIDTH // LANES):
        lanes = slice(p * LANES, (p + 1) * LANES)
        qp = q_ref[0, :, lanes]
        kp = k_ref[0, :, lanes]
        vp = v_ref[0, :, lanes]
        outs = []
        for e in range(2):
            qe = jnp.where(low if e == 0 else jnp.logical_not(low), qp, jnp.zeros_like(qp))
            outs.append(_softmax_pv([_dot_nt(qe, kp)], [vp]))
        o_ref[0, :, lanes] = jnp.where(low, outs[0], outs[1]).astype(BF16)


def _ctx_attn(q, k, v):
    b, lc, _ = q.shape
    spec = pl.BlockSpec((1, lc, NA_WIDTH), lambda bi: (bi, 0, 0))
    return pl.pallas_call(
        _ctx_attn_kernel,
        grid=(b,),
        in_specs=[spec, spec, spec],
        out_specs=spec,
        out_shape=jax.ShapeDtypeStruct((b, lc, NA_WIDTH), BF16),
        compiler_params=_params(("parallel",)),
        name="ctx_attn",
    )(q, k, v)


CONV_HALO = 16


CONV_TILE = 64


def _conv_window(h_ref, t0, n):
    t0 = pl.multiple_of(t0, CONV_TILE)
    before = h_ref[0, pl.ds(pl.multiple_of(jnp.maximum(t0 - CONV_HALO, 0), CONV_HALO), CONV_HALO), :]
    after = h_ref[0, pl.ds(pl.multiple_of(jnp.minimum(t0 + CONV_TILE, n - CONV_HALO), CONV_HALO), CONV_HALO), :]
    before = jnp.where(t0 > 0, before, 0.0)
    after = jnp.where(t0 + CONV_TILE < n, after, 0.0)
    return jnp.concatenate([before, h_ref[0, pl.ds(t0, CONV_TILE), :], after], axis=0)


def _conv_stages(h_ref, t0, tm, w_ref, cb_ref, lg_ref, lb_ref, pw_ref, pb_ref, out_ref, slot):
    shift = CONV_HALO - CONV_WIDTH // 2
    rows = CONV_TILE + 2 * CONV_HALO
    n = h_ref.shape[1]
    stages = []
    for s in range(tm // CONV_TILE):
        state = {}

        def taps(phases, s=s, state=state):
            chunk = _conv_window(h_ref, t0 + s * CONV_TILE, n)
            acc = state.get("acc", jnp.zeros((CONV_TILE, CONV_CH), F32))
            for phase in phases:
                off = (phase + shift) % 8
                base = phase + shift - off
                shifted = pltpu.roll(chunk, rows - off, 0) if off else chunk
                shifted = shifted[base:base + CONV_TILE + 8 * ((CONV_WIDTH - 1) // 8), :]
                for k in range(phase, CONV_WIDTH, 8):
                    acc = acc + shifted[k - phase:k - phase + CONV_TILE, :] * w_ref[k:k + 1, :]
            state["acc"] = acc

        def finish(s=s, state=state):
            hcv = state["acc"] + cb_ref[...]
            mu = jnp.mean(hcv, axis=-1, keepdims=True)
            ctr = hcv - mu
            var = jnp.mean(ctr * ctr, axis=-1, keepdims=True)
            y = _silu(ctr * lax.rsqrt(var + EPS) * lg_ref[...] + lb_ref[...])
            out = (_dot(y.astype(BF16), pw_ref[...]) + pb_ref[...]).astype(BF16)
            out_ref[slot, s * CONV_TILE:(s + 1) * CONV_TILE, :] = out

        stages.append(functools.partial(taps, range(0, 4)))
        stages.append(lambda taps=taps, finish=finish: (taps(range(4, 8)), finish()))
    return stages


GLA_INTRA_ROWS = 64


def _gla_intra(q_ref, k_ref, v_ref, c_ref, exp_ref, o_ref, r0, rowm, forward):
    sub = GLA_INTRA_ROWS
    q = q_ref[0, pl.ds(r0, sub), :]
    k = k_ref[0, pl.ds(r0, sub), :]
    v = v_ref[0, pl.ds(r0, sub), :]
    cum = c_ref[0, pl.ds(r0, sub), :]
    acc = _dot((q * k).astype(BF16), exp_ref[...]) * v

    def shift(t, d):
        return pltpu.roll(t, d if forward else sub - d, 0)

    for b in range(8):
        kb, cb, vb = (shift(k, b), shift(cum, b), shift(v, b)) if b else (k, cum, v)
        for d in (b, 8 + b):
            if d == 0:
                continue
            kd, cd, vd = (shift(kb, 8), shift(cb, 8), shift(vb, 8)) if d >= 8 else (kb, cb, vb)
            valid = (rowm >= d) if forward else (rowm < GLA_SUB - d)
            pd = q * kd * jnp.exp2(jnp.where(valid, cum - cd, MASK_NEG))
            acc = acc + _dot(pd.astype(BF16), exp_ref[...]) * vd
    o_ref[0, pl.ds(r0, sub), :] = acc


def _gla_prepare(q_ref, k_ref, v_ref, c_ref, t_ref, qt_ref, kh_ref, vt_ref):
    cum = c_ref[0]
    qt_ref[...] = (q_ref[0] * jnp.exp2(cum)).astype(BF16)
    kh_ref[...] = (k_ref[0] * jnp.exp2(t_ref[0] - cum)).astype(BF16)
    vt_ref[...] = v_ref[0].T.astype(BF16)


def _gla_updates(ib, vt_ref, kh_ref, u_ref, chunk_in_group):
    per_pass = GLA_INTRA_ROWS // GLA_SUB
    g0 = pl.multiple_of(((ib * GLA_INTRA_ROWS) // LANES) * LANES, LANES)
    first = (ib * per_pass) % (LANES // GLA_SUB)
    vtg = vt_ref[:, pl.ds(g0, LANES)]
    khg = kh_ref[pl.ds(g0, LANES), :]
    for c in range(per_pass):
        u_ref[ib * per_pass + c] = _dot(jnp.where(chunk_in_group == first + c, vtg, jnp.zeros((), BF16)), khg)


def _gla_recur(i, t_ref, s_ref, u_ref, sbf_ref, same_head):
    state = s_ref[...]
    sbf_ref[i] = jnp.where(same_head, state, 0.0).astype(BF16)
    r0 = pl.multiple_of(i * GLA_SUB, GLA_SUB)
    s_ref[...] = jnp.exp2(t_ref[0, pl.ds(r0, 1), :]) * state + u_ref[i]


def _gla_kernel(qf_ref, kf_ref, vf_ref, cf_ref, tf_ref, qb_ref, kb_ref, vb_ref, cb_ref, tb_ref, s0f_ref, s0b_ref,
                exp_ref, of_ref, ob_ref, sfo_ref, sbo_ref, sf_ref, sb_ref, qtf_ref, khf_ref, vtf_ref, uf_ref,
                sbff_ref, qtb_ref, khb_ref, vtb_ref, ub_ref, sbfb_ref):
    j = pl.program_id(1)
    tn = qf_ref.shape[1]
    n_chunks = tn // GLA_SUB

    @pl.when(j == 0)
    def _():
        sf_ref[...] = s0f_ref[0]
        sb_ref[...] = s0b_ref[0]

    _gla_prepare(qf_ref, kf_ref, vf_ref, cf_ref, tf_ref, qtf_ref, khf_ref, vtf_ref)
    _gla_prepare(qb_ref, kb_ref, vb_ref, cb_ref, tb_ref, qtb_ref, khb_ref, vtb_ref)

    rowm = lax.broadcasted_iota(jnp.int32, (GLA_INTRA_ROWS, GLA_K), 0) & (GLA_SUB - 1)
    chunk_in_group = lax.broadcasted_iota(jnp.int32, (1, LANES), 1) // GLA_SUB

    def intra(ib, carry):
        r0 = pl.multiple_of(ib * GLA_INTRA_ROWS, GLA_INTRA_ROWS)
        _gla_intra(qf_ref, kf_ref, vf_ref, cf_ref, exp_ref, of_ref, r0, rowm, True)
        _gla_intra(qb_ref, kb_ref, vb_ref, cb_ref, exp_ref, ob_ref, r0, rowm, False)
        _gla_updates(ib, vtf_ref, khf_ref, uf_ref, chunk_in_group)
        _gla_updates(ib, vtb_ref, khb_ref, ub_ref, chunk_in_group)
        return carry

    lax.fori_loop(0, tn // GLA_INTRA_ROWS, intra, 0)

    head_of_row = lax.broadcasted_iota(jnp.int32, (GLA_V, GLA_K), 0) // GLA_DV
    head_of_col = lax.broadcasted_iota(jnp.int32, (GLA_V, GLA_K), 1) // GLA_DK
    same_head = head_of_row == head_of_col

    def recur(i, carry):
        _gla_recur(i, tf_ref, sf_ref, uf_ref, sbff_ref, same_head)
        _gla_recur(n_chunks - 1 - i, tb_ref, sb_ref, ub_ref, sbfb_ref, same_head)
        return carry

    lax.fori_loop(0, n_chunks, recur, 0)

    for i in range(n_chunks):
        rows = slice(i * GLA_SUB, (i + 1) * GLA_SUB)
        of_ref[0, rows, :] = of_ref[0, rows, :] + _dot_nt(qtf_ref[rows, :], sbff_ref[i])
        ob_ref[0, rows, :] = ob_ref[0, rows, :] + _dot_nt(qtb_ref[rows, :], sbfb_ref[i])

    @pl.when(j == pl.num_programs(1) - 1)
    def _():
        sfo_ref[0] = sf_ref[...]
        sbo_ref[0] = sb_ref[...]


def _gla(qg, kg, vg, cf, tf, cb, tb, s0f, s0b, expand, tn):
    b, n, _ = qg.shape
    nt = n // tn
    fwd = lambda bi, j: (bi, j, 0)
    bwd = lambda bi, j: (bi, nt - 1 - j, 0)
    per_b = lambda bi, j: (bi, 0, 0)

    def spec(w, m):
        return pl.BlockSpec((1, tn, w), m)

    state_spec = pl.BlockSpec((1, GLA_V, GLA_K), per_b)
    return pl.pallas_call(
        _gla_kernel,
        grid=(b, nt),
        in_specs=[spec(GLA_K, fwd), spec(GLA_K, fwd), spec(GLA_V, fwd), spec(GLA_K, fwd), spec(GLA_K, fwd),
                  spec(GLA_K, bwd), spec(GLA_K, bwd), spec(GLA_V, bwd), spec(GLA_K, bwd), spec(GLA_K, bwd),
                  state_spec, state_spec,
                  pl.BlockSpec((GLA_K, GLA_V), lambda bi, j: (0, 0))],
        out_specs=[spec(GLA_V, fwd), spec(GLA_V, bwd), state_spec, state_spec],
        out_shape=[jax.ShapeDtypeStruct((b, n, GLA_V), F32), jax.ShapeDtypeStruct((b, n, GLA_V), F32),
                   jax.ShapeDtypeStruct((b, GLA_V, GLA_K), F32), jax.ShapeDtypeStruct((b, GLA_V, GLA_K), F32)],
        scratch_shapes=[pltpu.VMEM((GLA_V, GLA_K), F32), pltpu.VMEM((GLA_V, GLA_K), F32)] + 2 * [
            pltpu.VMEM((tn, GLA_K), BF16), pltpu.VMEM((tn, GLA_K), BF16), pltpu.VMEM((GLA_V, tn), BF16),
            pltpu.VMEM((tn // GLA_SUB, GLA_V, GLA_K), F32), pltpu.VMEM((tn // GLA_SUB, GLA_V, GLA_K), BF16)],
        compiler_params=_params(("parallel", "arbitrary")),
        name="gla",
    )(qg, kg, vg, cf, tf, qg, kg, vg, cb, tb, s0f, s0b, expand)


def _outmlp_kernel(x_ref, oa_ref, hc_ref, of_ref, obw_ref, sr_ref, mod_ref, gg_ref, ones_ref, wo_ref, g2_ref,
                   w1_ref, w2_ref, cw_ref, cb_ref, lg_ref, lb_ref, pw_ref, pb_ref, o_ref, ob_ref, *, tf):
    x = x_ref[0]
    tm = x.shape[0]
    ti = pl.program_id(1)

    def conv_stages(i, slot):
        return _conv_stages(hc_ref, i * tm, tm, cw_ref, cb_ref, lg_ref, lb_ref, pw_ref, pb_ref, ob_ref, slot)

    @pl.when(ti == 0)
    def _():
        for stage in conv_stages(ti, 0):
            stage()

    single_tile = hc_ref.shape[1] == tm
    next_stages = [] if single_tile else conv_stages(jnp.minimum(ti + 1, pl.num_programs(1) - 1), (ti + 1) % 2)
    ob = ob_ref[ti % 2]
    og = of_ref[0] + obw_ref[0]
    ss = _dot((og * og).astype(BF16), ones_ref[...])
    oc = (og * lax.rsqrt(ss * (1.0 / GLA_DV) + EPS) * gg_ref[...]) * sr_ref[0]
    c_b = NA_WIDTH
    c_c = NA_WIDTH + CONV_CH
    y = _dot(oa_ref[0], wo_ref[0:c_b, :]) + _dot(ob, wo_ref[c_b:c_c, :]) + _dot(oc.astype(BF16), wo_ref[c_c:, :])
    x1 = x + mod_ref[0, 2:3, :] * y
    ms = jnp.mean(x1 * x1, axis=-1, keepdims=True)
    h2 = x1 * lax.rsqrt(ms + EPS) * g2_ref[...]
    h2 = (h2 * (1.0 + mod_ref[0, 4:5, :]) + mod_ref[0, 3:4, :]).astype(BF16)
    acc = jnp.zeros(x.shape, F32)
    n_ff = w1_ref.shape[1] // tf
    slots = 2 * n_ff

    def run_stages(slot):
        for stage in next_stages[slot * len(next_stages) // slots:(slot + 1) * len(next_stages) // slots]:
            stage()

    for jf in range(n_ff):
        a = jnp.maximum(_dot(h2, w1_ref[:, jf * tf:(jf + 1) * tf]), 0.0)
        run_stages(2 * jf)
        acc = acc + _dot((a * a).astype(BF16), w2_ref[jf * tf:(jf + 1) * tf, :])
        run_stages(2 * jf + 1)
    o_ref[0] = x1 + mod_ref[0, 5:6, :] * acc


def _outmlp(x, oa, hc, of, obw, sr, mod, lw, tm):
    b, n, d = x.shape
    assert tm % CONV_TILE == 0 and n >= CONV_HALO
    vec = pl.BlockSpec((1, CONV_CH), lambda bi, ti: (0, 0))
    dff = lw["w1"].shape[1]
    tf = min(dff, 1024)
    mod_map = (lambda bi, ti: (bi, 0, 0)) if mod.shape[0] == b else (lambda bi, ti: (0, 0, 0))
    const = lambda bi, ti: (0, 0)
    tile = lambda bi, ti: (bi, ti, 0)
    once = pl.Buffered(1)
    return pl.pallas_call(
        functools.partial(_outmlp_kernel, tf=tf),
        grid=(b, n // tm),
        in_specs=[
            pl.BlockSpec((1, tm, d), tile),
            pl.BlockSpec((1, tm, NA_WIDTH), tile),
            pl.BlockSpec((1, n, CONV_CH), lambda bi, ti: (bi, 0, 0)),
            pl.BlockSpec((1, tm, GLA_V), tile),
            pl.BlockSpec((1, tm, GLA_V), tile),
            pl.BlockSpec((1, tm, GLA_V), tile),
            pl.BlockSpec((1, 6, d), mod_map),
            pl.BlockSpec((1, GLA_V), const),
            pl.BlockSpec((GLA_V, GLA_V), const),
            pl.BlockSpec((NA_WIDTH + CONV_CH + GLA_V, d), const, pipeline_mode=once),
            pl.BlockSpec((1, d), const),
            pl.BlockSpec((d, dff), const, pipeline_mode=once),
            pl.BlockSpec((dff, d), const, pipeline_mode=once),
            pl.BlockSpec((CONV_WIDTH, CONV_CH), const),
            vec, vec, vec,
            pl.BlockSpec((CONV_CH, CONV_CH), const),
            vec,
        ],
        out_specs=pl.BlockSpec((1, tm, d), tile),
        out_shape=jax.ShapeDtypeStruct((b, n, d), F32),
        scratch_shapes=[pltpu.VMEM((2, tm, CONV_CH), BF16)],
        compiler_params=_params(("parallel", "arbitrary")),
        name="outmlp",
    )(x, oa, hc, of, obw, sr, mod, lw["gg"], lw["ones_gla"], lw["w_out"], lw["g2"], lw["w1"], lw["w2"],
      lw["conv_w"], lw["conv_b"], lw["conv_ln_g"], lw["conv_ln_b"], lw["conv_pw_w"], lw["conv_pw_b"])


def _block_ones(n, group):
    idx = np.arange(n) // group
    return jnp.asarray(idx[:, None] == idx[None, :], BF16)


def _rope_tables(n):
    half = GLA_DK // 2
    n_freq = half // 2
    lane = np.arange(GLA_K) % GLA_DK
    use_col = lane >= half
    freq_idx = lane % n_freq
    second = (lane % half) >= n_freq
    pos = jnp.arange(n)
    inv_freq = ROPE_THETA ** (-jnp.arange(n_freq, dtype=F32) / n_freq)
    p = jnp.where(jnp.asarray(use_col)[None, :], (pos % GRID_W)[:, None], (pos // GRID_W)[:, None]).astype(F32)
    ang = p * inv_freq[jnp.asarray(freq_idx)][None, :]
    sign = jnp.where(jnp.asarray(second), 1.0, -1.0).astype(F32)
    return jnp.cos(ang), jnp.sin(ang) * sign[None, :]


def _layer_weights(l, w):
    d = w["w_in"].shape[1]
    gw = jnp.zeros((LANES, 2 * GLA_K), F32)
    gw = gw.at[:GLA_GATE_RANK, :GLA_K].set(w["gla_gw_f"][l])
    gw = gw.at[GLA_GATE_RANK:2 * GLA_GATE_RANK, GLA_K:].set(w["gla_gw_b"][l])
    row = lambda t: t.reshape(1, -1).astype(F32)
    return {
        "g1": row(w["norm1_g"][l]),
        "w_in": jnp.pad(w["w_in"][l], ((0, 0), (0, IN_WIDTH_PAD - IN_WIDTH))).astype(BF16),
        "ones_na": _block_ones(NA_WIDTH, NA_HEAD_DIM),
        "gq": row(jnp.tile(w["na_q_g"][l], NA_HEADS) * (NA_HEAD_DIM ** -0.5 * LOG2E)),
        "gk": row(jnp.tile(w["na_k_g"][l], NA_HEADS)),
        "gw": gw.astype(BF16),
        "gb": row(jnp.concatenate([w["gla_gb_f"][l], w["gla_gb_b"][l]])),
        "conv_w": w["conv_w"][l].astype(F32),
        "conv_b": row(w["conv_b"][l]),
        "conv_ln_g": row(w["conv_ln_g"][l]),
        "conv_ln_b": row(w["conv_ln_b"][l]),
        "conv_pw_w": w["conv_pw_w"][l].astype(BF16),
        "conv_pw_b": row(w["conv_pw_b"][l]),
        "gg": row(jnp.tile(w["gla_out_g"][l], GLA_HEADS)),
        "ones_gla": _block_ones(GLA_V, GLA_DV),
        "w_out": w["w_out"][l].astype(BF16),
        "g2": row(w["norm2_g"][l]),
        "w1": w["w_mlp_in"][l].astype(BF16),
        "w2": w["w_mlp_out"][l].astype(BF16),
    }


def kernel(x, c, ctx, c_ctx, w_ada, b_ada, norm1_g, w_in, na_q_g, na_k_g, na_rpb, conv_w, conv_b, conv_ln_g,
           conv_ln_b, conv_pw_w, conv_pw_b, gla_gw_f, gla_gb_f, gla_gw_b, gla_gb_b, gla_out_g, w_out, norm2_g,
           w_mlp_in, w_mlp_out):
    w = dict(norm1_g=norm1_g, w_in=w_in, na_q_g=na_q_g, na_k_g=na_k_g, conv_w=conv_w, conv_b=conv_b,
             conv_ln_g=conv_ln_g, conv_ln_b=conv_ln_b, conv_pw_w=conv_pw_w, conv_pw_b=conv_pw_b,
             gla_gw_f=gla_gw_f, gla_gb_f=gla_gb_f, gla_gw_b=gla_gw_b, gla_gb_b=gla_gb_b, gla_out_g=gla_out_g,
             w_out=w_out, norm2_g=norm2_g, w_mlp_in=w_mlp_in, w_mlp_out=w_mlp_out)
    b, n, d = x.shape
    lc = ctx.shape[1]
    depth = w_ada.shape[0]
    n_rows = n // GRID_W
    assert n % GRID_W == 0 and n_rows % NA_TILE_ROWS == 0 and n_rows >= NA_KEY_ROWS + 1
    tm_x = tm_c = tn_x = tn_c = 256
    tm_in = 512 if n % 512 == 0 else tm_x
    assert n % tm_x == 0 and lc % tm_c == 0

    ada_rows = -(-(b + 1) // 8) * 8
    cc = jnp.concatenate([c, c_ctx[None, :], jnp.zeros((ada_rows - b - 1, d), c.dtype)], axis=0).astype(F32)
    mod_all = _ada(cc, w_ada, b_ada)

    rope_tabs = _rope_tables(n)
    bias_tabs = _na_bias_tables(na_rpb, n_rows)
    expand = jnp.asarray((np.arange(GLA_K) // GLA_DK)[:, None] == (np.arange(GLA_V) // GLA_DV)[None, :], BF16)
    zero_state = jnp.zeros((b, GLA_V, GLA_K), F32)

    xc = ctx
    for l in range(depth):
        lw = _layer_weights(l, w)
        mod = mod_all[l].reshape(ada_rows, 6, d)
        mod_x = mod[:b]
        mod_c = mod[b:b + 1]
        need_ctx_out = l < depth - 1

        (qac, kac, vac, hcc, qgc, kgc, vgc, src, cfc, tfc, cbc, tbc) = _inproj(xc, mod_c, lw, None, tm_c)
        (qa, ka, va, hc, qg, kg, vg, sr, cf, tf, cb, tb) = _inproj(x, mod_x, lw, rope_tabs, tm_in)

        ofc, obc, s_f, s_b = _gla(qgc, kgc, vgc, cfc, tfc, cbc, tbc, zero_state, zero_state, expand, tn_c)
        of, obw, _, _ = _gla(qg, kg, vg, cf, tf, cb, tb, s_f, s_b, expand, tn_x)

        oa = _na_attn(qa, ka, va, kac, vac, bias_tabs[l], n_rows)
        x = _outmlp(x, oa, hc, of, obw, sr, mod_x, lw, tm_in)

        if need_ctx_out:
            oac = _ctx_attn(qac, kac, vac)
            xc = _outmlp(xc, oac, hcc, ofc, obc, src, mod_c, lw, tm_c)
    return x
```

```python
import functools

import numpy as np
import jax
import jax.numpy as jnp
from jax import lax
from jax.experimental import pallas as pl
from jax.experimental.pallas import tpu as pltpu

F32 = jnp.float32
BF16 = jnp.bfloat16

GRID_W = 64
EPS = 1e-6

NA_HEADS = 8
NA_HEAD_DIM = 64
NA_WIN_R = 8
NA_WIN_C = 16
NA_WIDTH = NA_HEADS * NA_HEAD_DIM

CONV_CH = 256
CONV_WIDTH = 31

GLA_HEADS = 4
GLA_DK = 32
GLA_DV = 64
GLA_GATE_RANK = 16
GLA_TAU = 16.0
GLA_K = GLA_HEADS * GLA_DK
GLA_V = GLA_HEADS * GLA_DV
ROPE_THETA = 10000.0

_C_QA, _C_KA, _C_VA, _C_UB = 0, NA_WIDTH, 2 * NA_WIDTH, 3 * NA_WIDTH
_C_QG = _C_UB + 2 * CONV_CH
_C_KG = _C_QG + GLA_K
_C_VG = _C_KG + GLA_K
_C_RG = _C_VG + GLA_V
_C_Z = _C_RG + GLA_V
IN_WIDTH = _C_Z + 2 * GLA_GATE_RANK
LANES = 128
IN_WIDTH_PAD = -(-IN_WIDTH // LANES) * LANES

GLA_SUB = 16
NA_TILE_ROWS = 4
NA_KEY_ROWS = NA_TILE_ROWS + NA_WIN_R - 1
MASK_NEG = -1e30
LOG2E = 1.4426950408889634
VMEM_LIMIT = 56 * 1024 * 1024


def _params(sem, vmem=VMEM_LIMIT):
    return pltpu.CompilerParams(dimension_semantics=sem, vmem_limit_bytes=vmem)


def _silu(t):
    return t * jax.nn.sigmoid(t)


def _dot(a, b):
    return jnp.dot(a, b, preferred_element_type=F32)


def _dot_nt(a, b):
    return lax.dot_general(a, b, (((1,), (1,)), ((), ())), preferred_element_type=F32)


def _ada_kernel(c_ref, w_ref, b_ref, o_ref):
    s = _silu(c_ref[...]).astype(BF16)
    o_ref[0] = _dot(s, w_ref[0].astype(BF16)) + b_ref[0]


def _ada(cc, w_ada, b_ada):
    depth, d, d6 = w_ada.shape
    rows = cc.shape[0]
    tn = 1536 if d6 % 1536 == 0 else d6
    return pl.pallas_call(
        _ada_kernel,
        grid=(depth, d6 // tn),
        in_specs=[
            pl.BlockSpec((rows, d), lambda l, n: (0, 0)),
            pl.BlockSpec((1, d, tn), lambda l, n: (l, 0, n)),
            pl.BlockSpec((1, 1, tn), lambda l, n: (l, 0, n)),
        ],
        out_specs=pl.BlockSpec((1, rows, tn), lambda l, n: (l, 0, n)),
        out_shape=jax.ShapeDtypeStruct((depth, rows, d6), F32),
        compiler_params=_params(("parallel", "parallel")),
        name="ada",
    )(cc, w_ada, b_ada.reshape(depth, 1, d6))


def _block_scan(t, rowm, reverse):
    n = t.shape[0]
    s = 1
    while s < GLA_SUB:
        if reverse:
            t = t + jnp.where(rowm < GLA_SUB - s, pltpu.roll(t, n - s, 0), 0.0)
        else:
            t = t + jnp.where(rowm >= s, pltpu.roll(t, s, 0), 0.0)
        s *= 2
    return t


def _inproj_kernel(*refs, rope):
    if rope:
        (x_ref, mod_ref, g1_ref, w_ref, ones_ref, gq_ref, gk_ref, gw_ref, gb_ref, cos_ref, sin_ref,
         qa_ref, ka_ref, va_ref, hc_ref, qg_ref, kg_ref, vg_ref, sr_ref, cf_ref, tf_ref, cb_ref, tb_ref) = refs
    else:
        (x_ref, mod_ref, g1_ref, w_ref, ones_ref, gq_ref, gk_ref, gw_ref, gb_ref,
         qa_ref, ka_ref, va_ref, hc_ref, qg_ref, kg_ref, vg_ref, sr_ref, cf_ref, tf_ref, cb_ref, tb_ref) = refs
    x = x_ref[0]
    ms = jnp.mean(x * x, axis=-1, keepdims=True)
    h = x * lax.rsqrt(ms + EPS) * g1_ref[...]
    hb = (h * (1.0 + mod_ref[0, 1:2, :]) + mod_ref[0, 0:1, :]).astype(BF16)

    def seg(lo, hi):
        return _dot(hb, w_ref[:, lo:hi])

    def head_norm(t, g_ref):
        ss = _dot((t * t).astype(BF16), ones_ref[...])
        return (t * lax.rsqrt(ss * (1.0 / NA_HEAD_DIM) + EPS) * g_ref[...]).astype(BF16)

    qa_ref[0] = head_norm(seg(_C_QA, _C_KA), gq_ref)
    ka_ref[0] = head_norm(seg(_C_KA, _C_VA), gk_ref)
    va_ref[0] = seg(_C_VA, _C_UB).astype(BF16)

    ub = seg(_C_UB, _C_QG)
    hc_ref[0] = ub[:, :CONV_CH] * jax.nn.sigmoid(ub[:, CONV_CH:])

    qkg = seg(_C_QG, _C_VG)
    qg = qkg[:, :GLA_K]
    kg = qkg[:, GLA_K:]
    if rope:
        cos = cos_ref[...]
        sin = sin_ref[...]
        first = (lax.broadcasted_iota(jnp.int32, (1, GLA_K), 1) & 15) < 8

        def rot(t):
            partner = jnp.where(first, pltpu.roll(t, GLA_K - 8, 1), pltpu.roll(t, 8, 1))
            return t * cos + partner * sin

        qg = rot(qg)
        kg = rot(kg)
    qg_ref[0] = qg * (GLA_DK ** -0.5)
    kg_ref[0] = kg
    vg_ref[0] = seg(_C_VG, _C_RG)
    sr_ref[0] = _silu(seg(_C_RG, _C_Z))

    z = seg(_C_Z, IN_WIDTH_PAD).astype(BF16)
    a = _dot(z, gw_ref[...]) + gb_ref[...]
    la = (jnp.minimum(a, 0.0) - jnp.log(1.0 + jnp.exp(-jnp.abs(a)))) * (LOG2E / GLA_TAU)
    laf = la[:, :GLA_K]
    lab = la[:, GLA_K:]
    rowm = lax.broadcasted_iota(jnp.int32, laf.shape, 0) & (GLA_SUB - 1)
    pf = _block_scan(laf, rowm, False)
    sf = _block_scan(laf, rowm, True)
    pb = _block_scan(lab, rowm, False)
    sb = _block_scan(lab, rowm, True)
    cf_ref[0] = pf
    tf_ref[0] = pf + sf - laf
    cb_ref[0] = sb
    tb_ref[0] = pb + sb - lab


def _inproj(x, mod, lw, rope_tabs, tm):
    b, n, d = x.shape
    nt = n // tm
    rope = rope_tabs is not None
    mod_map = (lambda bi, ti: (bi, 0, 0)) if mod.shape[0] == b else (lambda bi, ti: (0, 0, 0))
    const = lambda bi, ti: (0, 0)
    tile = lambda bi, ti: (bi, ti, 0)
    in_specs = [
        pl.BlockSpec((1, tm, d), tile),
        pl.BlockSpec((1, 6, d), mod_map),
        pl.BlockSpec((1, d), const),
        pl.BlockSpec((d, IN_WIDTH_PAD), const),
        pl.BlockSpec((NA_WIDTH, NA_WIDTH), const),
        pl.BlockSpec((1, NA_WIDTH), const),
        pl.BlockSpec((1, NA_WIDTH), const),
        pl.BlockSpec((LANES, 2 * GLA_K), const),
        pl.BlockSpec((1, 2 * GLA_K), const),
    ]
    args = [x, mod, lw["g1"], lw["w_in"], lw["ones_na"], lw["gq"], lw["gk"], lw["gw"], lw["gb"]]
    if rope:
        in_specs += [pl.BlockSpec((tm, GLA_K), lambda bi, ti: (ti, 0))] * 2
        args += list(rope_tabs)
    widths = [(NA_WIDTH, BF16)] * 3 + [(CONV_CH, F32), (GLA_K, F32), (GLA_K, F32), (GLA_V, F32), (GLA_V, F32)] + [
        (GLA_K, F32)] * 4
    out_specs = [pl.BlockSpec((1, tm, w), tile) for w, _ in widths]
    out_shape = [jax.ShapeDtypeStruct((b, n, w), dt) for w, dt in widths]
    return pl.pallas_call(
        functools.partial(_inproj_kernel, rope=rope),
        grid=(b, nt),
        in_specs=in_specs,
        out_specs=out_specs,
        out_shape=out_shape,
        compiler_params=_params(("parallel", "parallel")),
        name="inproj_rope" if rope else "inproj",
    )(*args)


def _softmax_pv(s_list, v_list):
    m = s_list[0].max(axis=-1, keepdims=True)
    for s in s_list[1:]:
        m = jnp.maximum(m, s.max(axis=-1, keepdims=True))
    l = 0.0
    o = 0.0
    for s, v in zip(s_list, v_list):
        p = jnp.exp2(s - m)
        l = l + p.sum(axis=-1, keepdims=True)
        o = o + _dot(p.astype(BF16), v)
    return o / l


def _na_kernel(q_ref, k_ref, v_ref, kc_ref, vc_ref, bias_ref, o_ref, sl_ref, sc_ref, *, n_rows):
    j = pl.program_id(1)
    w0 = jnp.clip(NA_TILE_ROWS * j - NA_WIN_R // 2, 0, n_rows - NA_KEY_ROWS)
    start = pl.multiple_of(w0 * GRID_W, GRID_W)
    keys = pl.ds(start, NA_KEY_ROWS * GRID_W)
    low = lax.broadcasted_iota(jnp.int32, (1, LANES), 1) < NA_HEAD_DIM
    one = jnp.ones((), BF16)
    pattern = jnp.where(j == 0, 0, jnp.where(j == pl.num_programs(1) - 1, 2, 1))

    def scores(h):
        lanes = slice((h // 2) * LANES, (h // 2 + 1) * LANES)
        qp = q_ref[0, :, lanes]
        qe = jnp.where(low if h % 2 == 0 else jnp.logical_not(low), qp, jnp.zeros_like(qp))
        sl_ref[h % 2] = _dot_nt(qe, k_ref[0, keys, lanes]) + bias_ref[pattern, h]
        sc_ref[h % 2] = _dot_nt(qe, kc_ref[0, :, lanes])

    def attend(h):
        lanes = slice((h // 2) * LANES, (h // 2 + 1) * LANES)
        mine = low if h % 2 == 0 else jnp.logical_not(low)
        s_loc = sl_ref[h % 2]
        s_ctx = sc_ref[h % 2]
        m = jnp.maximum(s_loc.max(axis=-1, keepdims=True), s_ctx.max(axis=-1, keepdims=True))
        v_loc = jnp.where(mine, v_ref[0, keys, lanes], one)
        v_ctx = jnp.where(mine, vc_ref[0, :, lanes], one)
        full = (_dot(jnp.exp2(s_loc - m).astype(BF16), v_loc)
                + _dot(jnp.exp2(s_ctx - m).astype(BF16), v_ctx))
        return full / pltpu.roll(full, NA_HEAD_DIM, 1)

    scores(0)
    for p in range(NA_WIDTH // LANES):
        outs = []
        for e in range(2):
            h = 2 * p + e
            if h + 1 < NA_HEADS:
                scores(h + 1)
            outs.append(attend(h))
        o_ref[0, :, p * LANES:(p + 1) * LANES] = jnp.where(low, outs[0], outs[1]).astype(BF16)


def _na_bias_tables(rpb, n_rows):
    n_tiles = n_rows // NA_TILE_ROWS
    rpb = rpb.reshape((-1,) + rpb.shape[2:])
    cq = np.arange(GRID_W)
    cs = np.clip(cq - NA_WIN_C // 2, 0, GRID_W - NA_WIN_C)
    ck = np.arange(GRID_W)[None, :]
    col_valid = (ck >= cs[:, None]) & (ck < cs[:, None] + NA_WIN_C)
    padded = jnp.pad(rpb.astype(F32), ((0, 0), (0, 0), (GRID_W, GRID_W)))
    rows_q = [padded[:, :, GRID_W + NA_WIN_C - 1 - q:GRID_W + NA_WIN_C - 1 - q + GRID_W] for q in range(GRID_W)]
    blocks = jnp.stack(rows_q, axis=2)
    blocks = jnp.where(jnp.asarray(col_valid)[None, None], blocks * LOG2E, MASK_NEG)
    masked_block = jnp.full((rpb.shape[0], GRID_W, GRID_W), MASK_NEG, F32)
    pats = []
    for jt in (0, 1, n_tiles - 1):
        rq0 = jt * NA_TILE_ROWS
        w0 = int(np.clip(rq0 - NA_WIN_R // 2, 0, n_rows - NA_KEY_ROWS))
        q_rows = []
        for ri in range(NA_TILE_ROWS):
            r = rq0 + ri
            r0 = int(np.clip(r - NA_WIN_R // 2, 0, n_rows - NA_WIN_R))
            k_blocks = []
            for rki in range(NA_KEY_ROWS):
                rk = w0 + rki
                k_blocks.append(blocks[:, rk - r + NA_WIN_R - 1] if r0 <= rk < r0 + NA_WIN_R else masked_block)
            q_rows.append(jnp.concatenate(k_blocks, axis=-1))
        pats.append(jnp.concatenate(q_rows, axis=-2))
    return jnp.stack(pats)


def _na_attn(qa, ka, va, kc, vc, bias_tabs, layer, n_rows):
    b, n, _ = qa.shape
    lc = kc.shape[1]
    n_tiles = n_rows // NA_TILE_ROWS
    tq = NA_TILE_ROWS * GRID_W
    nk = NA_KEY_ROWS * GRID_W

    return pl.pallas_call(
        functools.partial(_na_kernel, n_rows=n_rows),
        grid=(b, n_tiles),
        in_specs=[
            pl.BlockSpec((1, tq, NA_WIDTH), lambda bi, j: (bi, j, 0)),
            pl.BlockSpec((1, n, NA_WIDTH), lambda bi, j: (bi, 0, 0)),
            pl.BlockSpec((1, n, NA_WIDTH), lambda bi, j: (bi, 0, 0)),
            pl.BlockSpec((1, lc, NA_WIDTH), lambda bi, j: (bi, 0, 0)),
            pl.BlockSpec((1, lc, NA_WIDTH), lambda bi, j: (bi, 0, 0)),
            pl.BlockSpec((3, NA_HEADS, tq, nk), lambda bi, j: (0, layer, 0, 0), pipeline_mode=pl.Buffered(1)),
        ],
        out_specs=pl.BlockSpec((1, tq, NA_WIDTH), lambda bi, j: (bi, j, 0)),
        out_shape=jax.ShapeDtypeStruct((b, n, NA_WIDTH), BF16),
        scratch_shapes=[pltpu.VMEM((2, tq, nk), F32), pltpu.VMEM((2, tq, lc), F32)],
        compiler_params=_params(("parallel", "arbitrary")),
        name="na_attn",
    )(qa, ka, va, kc, vc, bias_tabs)


def _ctx_attn_kernel(q_ref, k_ref, v_ref, o_ref):
    low = lax.broadcasted_iota(jnp.int32, (1, LANES), 1) < NA_HEAD_DIM
    for p in range(NA_WIDTH // LANES):
        lanes = slice(p * LANES, (p + 1) * LANES)
        qp = q_ref[0, :, lanes]
        kp = k_ref[0, :, lanes]
        vp = v_ref[0, :, lanes]
        outs = []
        for e in range(2):
            qe = jnp.where(low if e == 0 else jnp.logical_not(low), qp, jnp.zeros_like(qp))
            outs.append(_softmax_pv([_dot_nt(qe, kp)], [vp]))
        o_ref[0, :, lanes] = jnp.where(low, outs[0], outs[1]).astype(BF16)


def _ctx_attn(q, k, v):
    b, lc, _ = q.shape
    spec = pl.BlockSpec((1, lc, NA_WIDTH), lambda bi: (bi, 0, 0))
    return pl.pallas_call(
        _ctx_attn_kernel,
        grid=(b,),
        in_specs=[spec, spec, spec],
        out_specs=spec,
        out_shape=jax.ShapeDtypeStruct((b, lc, NA_WIDTH), BF16),
        compiler_params=_params(("parallel",)),
        name="ctx_attn",
    )(q, k, v)


CONV_HALO = 16


CONV_TILE = 64


def _conv_window(h_ref, t0, n):
    t0 = pl.multiple_of(t0, CONV_TILE)
    before = h_ref[0, pl.ds(pl.multiple_of(jnp.maximum(t0 - CONV_HALO, 0), CONV_HALO), CONV_HALO), :]
    after = h_ref[0, pl.ds(pl.multiple_of(jnp.minimum(t0 + CONV_TILE, n - CONV_HALO), CONV_HALO), CONV_HALO), :]
    before = jnp.where(t0 > 0, before, 0.0)
    after = jnp.where(t0 + CONV_TILE < n, after, 0.0)
    return jnp.concatenate([before, h_ref[0, pl.ds(t0, CONV_TILE), :], after], axis=0)


def _conv_stages(h_ref, t0, tm, w_ref, cb_ref, lg_ref, lb_ref, pw_ref, pb_ref, out_ref, slot):
    shift = CONV_HALO - CONV_WIDTH // 2
    rows = CONV_TILE + 2 * CONV_HALO
    n = h_ref.shape[1]
    stages = []
    for s in range(tm // CONV_TILE):
        state = {}

        def taps(phases, s=s, state=state):
            chunk = _conv_window(h_ref, t0 + s * CONV_TILE, n)
            acc = state.get("acc", jnp.zeros((CONV_TILE, CONV_CH), F32))
            for phase in phases:
                off = (phase + shift) % 8
                base = phase + shift - off
                shifted = pltpu.roll(chunk, rows - off, 0) if off else chunk
                shifted = shifted[base:base + CONV_TILE + 8 * ((CONV_WIDTH - 1) // 8), :]
                for k in range(phase, CONV_WIDTH, 8):
                    acc = acc + shifted[k - phase:k - phase + CONV_TILE, :] * w_ref[k:k + 1, :]
            state["acc"] = acc

        def finish(s=s, state=state):
            hcv = state["acc"] + cb_ref[...]
            mu = jnp.mean(hcv, axis=-1, keepdims=True)
            ctr = hcv - mu
            var = jnp.mean(ctr * ctr, axis=-1, keepdims=True)
            y = _silu(ctr * lax.rsqrt(var + EPS) * lg_ref[...] + lb_ref[...])
            out = (_dot(y.astype(BF16), pw_ref[...]) + pb_ref[...]).astype(BF16)
            out_ref[slot, s * CONV_TILE:(s + 1) * CONV_TILE, :] = out

        stages.append(functools.partial(taps, range(0, 4)))
        stages.append(lambda taps=taps, finish=finish: (taps(range(4, 8)), finish()))
    return stages


GLA_INTRA_ROWS = 64


def _gla_intra_stages(q_ref, k_ref, v_ref, c_ref, sel_ref, rep_ref, o_ref, p_ref, w_ref, sp_ref, slot, r0, rowm,
                       forward):
    sub = GLA_INTRA_ROWS
    per = sub // GLA_SUB
    rows = slice(r0, r0 + sub)

    def products():
        q = q_ref[0, rows, :]
        k = k_ref[0, rows, :]
        cum = c_ref[0, rows, :]

        def source_row(t, s):
            t3 = t.reshape(per, GLA_SUB, t.shape[-1])
            return jnp.broadcast_to(t3[:, s:s + 1, :], t3.shape).reshape(t.shape)

        for s in range(GLA_SUB):
            valid = (rowm >= s) if forward else (rowm <= s)
            p = q * source_row(k, s) * jnp.exp2(jnp.where(valid, cum - source_row(cum, s), MASK_NEG))
            p_ref[slot, :, s * GLA_K:(s + 1) * GLA_K] = p.astype(BF16)

    def weights():
        w_ref[slot] = _dot(p_ref[slot], sel_ref[...]).astype(BF16)

    def spread():
        full = _dot(w_ref[slot], rep_ref[...])
        row_chunk = lax.broadcasted_iota(jnp.int32, full.shape, 0) // GLA_SUB
        lane_chunk = lax.broadcasted_iota(jnp.int32, full.shape, 1) // (GLA_HEADS * GLA_SUB)
        sp_ref[slot] = jnp.where(row_chunk == lane_chunk, full, 0.0).astype(BF16)

    def apply():
        v = v_ref[0, rows, :]
        lane_head = lax.broadcasted_iota(jnp.int32, (GLA_SUB, GLA_V), 1) // GLA_DV
        pieces = []
        for c in range(per):
            chunk = v[c * GLA_SUB:(c + 1) * GLA_SUB, :]
            pieces += [jnp.where(lane_head == h, chunk, 0.0) for h in range(GLA_HEADS)]
        o_ref[0, rows, :] = _dot(sp_ref[slot], jnp.concatenate(pieces, axis=0).astype(BF16))

    return products, weights, spread, apply


def _gla_prepare(q_ref, k_ref, v_ref, c_ref, t_ref, qt_ref, kh_ref, vt_ref):
    cum = c_ref[0]
    qt_ref[...] = (q_ref[0] * jnp.exp2(cum)).astype(BF16)
    kh_ref[...] = (k_ref[0] * jnp.exp2(t_ref[0] - cum)).astype(BF16)
    vt_ref[...] = v_ref[0].T.astype(BF16)


def _gla_updates(ib, vt_ref, kh_ref, u_ref, chunk_in_group):
    per_pass = GLA_INTRA_ROWS // GLA_SUB
    g0 = ((ib * GLA_INTRA_ROWS) // LANES) * LANES
    first = (ib * per_pass) % (LANES // GLA_SUB)
    vtg = vt_ref[:, g0:g0 + LANES]
    khg = kh_ref[g0:g0 + LANES, :]
    for c in range(per_pass):
        u_ref[ib * per_pass + c] = _dot(jnp.where(chunk_in_group == first + c, vtg, jnp.zeros((), BF16)), khg)


def _gla_recur(i, t_ref, s_ref, u_ref, sbf_ref, same_head):
    state = s_ref[...]
    sbf_ref[i] = jnp.where(same_head, state, 0.0).astype(BF16)
    s_ref[...] = jnp.exp2(t_ref[0, i * GLA_SUB:i * GLA_SUB + 1, :]) * state + u_ref[i]


def _gla_kernel(qf_ref, kf_ref, vf_ref, cf_ref, tf_ref, qb_ref, kb_ref, vb_ref, cb_ref, tb_ref, s0f_ref, s0b_ref,
                sel_ref, rep_ref, of_ref, ob_ref, sfo_ref, sbo_ref, sf_ref, sb_ref, qtf_ref, khf_ref, vtf_ref, uf_ref,
                sbff_ref, qtb_ref, khb_ref, vtb_ref, ub_ref, sbfb_ref, p_ref, w_ref, sp_ref):
    j = pl.program_id(1)
    tn = qf_ref.shape[1]
    n_chunks = tn // GLA_SUB

    @pl.when(j == 0)
    def _():
        sf_ref[...] = s0f_ref[0]
        sb_ref[...] = s0b_ref[0]

    _gla_prepare(qf_ref, kf_ref, vf_ref, cf_ref, tf_ref, qtf_ref, khf_ref, vtf_ref)
    _gla_prepare(qb_ref, kb_ref, vb_ref, cb_ref, tb_ref, qtb_ref, khb_ref, vtb_ref)

    rowm = lax.broadcasted_iota(jnp.int32, (GLA_INTRA_ROWS, GLA_K), 0) & (GLA_SUB - 1)
    chunk_in_group = lax.broadcasted_iota(jnp.int32, (1, LANES), 1) // GLA_SUB

    blocks = []
    for ib in range(tn // GLA_INTRA_ROWS):
        r0 = ib * GLA_INTRA_ROWS
        blocks.append(_gla_intra_stages(qf_ref, kf_ref, vf_ref, cf_ref, sel_ref, rep_ref, of_ref, p_ref, w_ref,
                                        sp_ref, 2 * ib, r0, rowm, True))
        blocks.append(_gla_intra_stages(qb_ref, kb_ref, vb_ref, cb_ref, sel_ref, rep_ref, ob_ref, p_ref, w_ref,
                                        sp_ref, 2 * ib + 1, r0, rowm, False))
    for ib in range(tn // GLA_INTRA_ROWS):
        blocks[2 * ib][0]()
        _gla_updates(ib, vtf_ref, khf_ref, uf_ref, chunk_in_group)
        blocks[2 * ib + 1][0]()
        _gla_updates(ib, vtb_ref, khb_ref, ub_ref, chunk_in_group)
    for stage in range(1, 4):
        for block in blocks:
            block[stage]()

    head_of_row = lax.broadcasted_iota(jnp.int32, (GLA_V, GLA_K), 0) // GLA_DV
    head_of_col = lax.broadcasted_iota(jnp.int32, (GLA_V, GLA_K), 1) // GLA_DK
    same_head = head_of_row == head_of_col

    for i in range(n_chunks):
        for c, t_ref, s_ref, u_ref, sbf_ref, qt_ref, o_ref in (
                (i, tf_ref, sf_ref, uf_ref, sbff_ref, qtf_ref, of_ref),
                (n_chunks - 1 - i, tb_ref, sb_ref, ub_ref, sbfb_ref, qtb_ref, ob_ref)):
            _gla_recur(c, t_ref, s_ref, u_ref, sbf_ref, same_head)
            rows = slice(c * GLA_SUB, (c + 1) * GLA_SUB)
            o_ref[0, rows, :] = o_ref[0, rows, :] + _dot_nt(qt_ref[rows, :], sbf_ref[c])

    @pl.when(j == pl.num_programs(1) - 1)
    def _():
        sfo_ref[0] = sf_ref[...]
        sbo_ref[0] = sb_ref[...]


def _gla_constants():
    head_of_chan = np.arange(GLA_K) // GLA_DK
    lane = np.arange(GLA_HEADS * GLA_SUB)
    sel = np.concatenate([(head_of_chan[:, None] == lane[None, :] // GLA_SUB) & (lane[None, :] % GLA_SUB == s)
                          for s in range(GLA_SUB)])
    per = GLA_INTRA_ROWS // GLA_SUB
    rep = lane[:, None] == (np.arange(per * lane.size) % lane.size)[None, :]
    return jnp.asarray(sel, BF16), jnp.asarray(rep, BF16)


def _gla(qg, kg, vg, cf, tf, cb, tb, s0f, s0b, consts, tn):
    b, n, _ = qg.shape
    sel, rep = consts
    nt = n // tn
    fwd = lambda bi, j: (bi, j, 0)
    bwd = lambda bi, j: (bi, nt - 1 - j, 0)
    per_b = lambda bi, j: (bi, 0, 0)

    def spec(w, m):
        return pl.BlockSpec((1, tn, w), m)

    state_spec = pl.BlockSpec((1, GLA_V, GLA_K), per_b)
    return pl.pallas_call(
        _gla_kernel,
        grid=(b, nt),
        in_specs=[spec(GLA_K, fwd), spec(GLA_K, fwd), spec(GLA_V, fwd), spec(GLA_K, fwd), spec(GLA_K, fwd),
                  spec(GLA_K, bwd), spec(GLA_K, bwd), spec(GLA_V, bwd), spec(GLA_K, bwd), spec(GLA_K, bwd),
                  state_spec, state_spec,
                  pl.BlockSpec(sel.shape, lambda bi, j: (0, 0)),
                  pl.BlockSpec(rep.shape, lambda bi, j: (0, 0))],
        out_specs=[spec(GLA_V, fwd), spec(GLA_V, bwd), state_spec, state_spec],
        out_shape=[jax.ShapeDtypeStruct((b, n, GLA_V), F32), jax.ShapeDtypeStruct((b, n, GLA_V), F32),
                   jax.ShapeDtypeStruct((b, GLA_V, GLA_K), F32), jax.ShapeDtypeStruct((b, GLA_V, GLA_K), F32)],
        scratch_shapes=[pltpu.VMEM((GLA_V, GLA_K), F32), pltpu.VMEM((GLA_V, GLA_K), F32)] + 2 * [
            pltpu.VMEM((tn, GLA_K), BF16), pltpu.VMEM((tn, GLA_K), BF16), pltpu.VMEM((GLA_V, tn), BF16),
            pltpu.VMEM((tn // GLA_SUB, GLA_V, GLA_K), F32), pltpu.VMEM((tn // GLA_SUB, GLA_V, GLA_K), BF16)] + [
            pltpu.VMEM((2 * tn // GLA_INTRA_ROWS, GLA_INTRA_ROWS, GLA_SUB * GLA_K), BF16),
            pltpu.VMEM((2 * tn // GLA_INTRA_ROWS, GLA_INTRA_ROWS, GLA_HEADS * GLA_SUB), BF16),
            pltpu.VMEM((2 * tn // GLA_INTRA_ROWS, GLA_INTRA_ROWS, GLA_V), BF16)],
        compiler_params=_params(("parallel", "arbitrary")),
        name="gla",
    )(qg, kg, vg, cf, tf, qg, kg, vg, cb, tb, s0f, s0b, sel, rep)


def _outmlp_kernel(x_ref, oa_ref, hc_ref, of_ref, obw_ref, sr_ref, mod_ref, gg_ref, ones_ref, wo_ref, g2_ref,
                   w1_ref, w2_ref, cw_ref, cb_ref, lg_ref, lb_ref, pw_ref, pb_ref, o_ref, ob_ref, *, tf):
    x = x_ref[0]
    tm = x.shape[0]
    ti = pl.program_id(1)

    def conv_stages(i, slot):
        return _conv_stages(hc_ref, i * tm, tm, cw_ref, cb_ref, lg_ref, lb_ref, pw_ref, pb_ref, ob_ref, slot)

    @pl.when(ti == 0)
    def _():
        for stage in conv_stages(ti, 0):
            stage()

    single_tile = hc_ref.shape[1] == tm
    next_stages = [] if single_tile else conv_stages(jnp.minimum(ti + 1, pl.num_programs(1) - 1), (ti + 1) % 2)
    ob = ob_ref[ti % 2]
    og = of_ref[0] + obw_ref[0]
    ss = _dot((og * og).astype(BF16), ones_ref[...])
    oc = (og * lax.rsqrt(ss * (1.0 / GLA_DV) + EPS) * gg_ref[...]) * sr_ref[0]
    c_b = NA_WIDTH
    c_c = NA_WIDTH + CONV_CH
    y = _dot(oa_ref[0], wo_ref[0:c_b, :]) + _dot(ob, wo_ref[c_b:c_c, :]) + _dot(oc.astype(BF16), wo_ref[c_c:, :])
    x1 = x + mod_ref[0, 2:3, :] * y
    ms = jnp.mean(x1 * x1, axis=-1, keepdims=True)
    h2 = x1 * lax.rsqrt(ms + EPS) * g2_ref[...]
    h2 = (h2 * (1.0 + mod_ref[0, 4:5, :]) + mod_ref[0, 3:4, :]).astype(BF16)
    acc = jnp.zeros(x.shape, F32)
    n_ff = w1_ref.shape[1] // tf
    slots = 2 * n_ff

    def run_stages(slot):
        for stage in next_stages[slot * len(next_stages) // slots:(slot + 1) * len(next_stages) // slots]:
            stage()

    for jf in range(n_ff):
        a = jnp.maximum(_dot(h2, w1_ref[:, jf * tf:(jf + 1) * tf]), 0.0)
        run_stages(2 * jf)
        acc = acc + _dot((a * a).astype(BF16), w2_ref[jf * tf:(jf + 1) * tf, :])
        run_stages(2 * jf + 1)
    o_ref[0] = x1 + mod_ref[0, 5:6, :] * acc


def _outmlp(x, oa, hc, of, obw, sr, mod, lw, tm):
    b, n, d = x.shape
    assert tm % CONV_TILE == 0 and n >= CONV_HALO
    vec = pl.BlockSpec((1, CONV_CH), lambda bi, ti: (0, 0))
    dff = lw["w1"].shape[1]
    tf = min(dff, 1024)
    mod_map = (lambda bi, ti: (bi, 0, 0)) if mod.shape[0] == b else (lambda bi, ti: (0, 0, 0))
    const = lambda bi, ti: (0, 0)
    tile = lambda bi, ti: (bi, ti, 0)
    once = pl.Buffered(1)
    return pl.pallas_call(
        functools.partial(_outmlp_kernel, tf=tf),
        grid=(b, n // tm),
        in_specs=[
            pl.BlockSpec((1, tm, d), tile),
            pl.BlockSpec((1, tm, NA_WIDTH), tile),
            pl.BlockSpec((1, n, CONV_CH), lambda bi, ti: (bi, 0, 0)),
            pl.BlockSpec((1, tm, GLA_V), tile),
            pl.BlockSpec((1, tm, GLA_V), tile),
            pl.BlockSpec((1, tm, GLA_V), tile),
            pl.BlockSpec((1, 6, d), mod_map),
            pl.BlockSpec((1, GLA_V), const),
            pl.BlockSpec((GLA_V, GLA_V), const),
            pl.BlockSpec((NA_WIDTH + CONV_CH + GLA_V, d), const, pipeline_mode=once),
            pl.BlockSpec((1, d), const),
            pl.BlockSpec((d, dff), const, pipeline_mode=once),
            pl.BlockSpec((dff, d), const, pipeline_mode=once),
            pl.BlockSpec((CONV_WIDTH, CONV_CH), const),
            vec, vec, vec,
            pl.BlockSpec((CONV_CH, CONV_CH), const),
            vec,
        ],
        out_specs=pl.BlockSpec((1, tm, d), tile),
        out_shape=jax.ShapeDtypeStruct((b, n, d), F32),
        scratch_shapes=[pltpu.VMEM((2, tm, CONV_CH), BF16)],
        compiler_params=_params(("parallel", "arbitrary")),
        name="outmlp",
    )(x, oa, hc, of, obw, sr, mod, lw["gg"], lw["ones_gla"], lw["w_out"], lw["g2"], lw["w1"], lw["w2"],
      lw["conv_w"], lw["conv_b"], lw["conv_ln_g"], lw["conv_ln_b"], lw["conv_pw_w"], lw["conv_pw_b"])


def _block_ones(n, group):
    idx = np.arange(n) // group
    return jnp.asarray(idx[:, None] == idx[None, :], BF16)


def _rope_tables(n):
    half = GLA_DK // 2
    n_freq = half // 2
    lane = np.arange(GLA_K) % GLA_DK
    use_col = lane >= half
    freq_idx = lane % n_freq
    second = (lane % half) >= n_freq
    pos = jnp.arange(n)
    inv_freq = ROPE_THETA ** (-jnp.arange(n_freq, dtype=F32) / n_freq)
    p = jnp.where(jnp.asarray(use_col)[None, :], (pos % GRID_W)[:, None], (pos // GRID_W)[:, None]).astype(F32)
    ang = p * inv_freq[jnp.asarray(freq_idx)][None, :]
    sign = jnp.where(jnp.asarray(second), 1.0, -1.0).astype(F32)
    return jnp.cos(ang), jnp.sin(ang) * sign[None, :]


def _layer_weights(l, w):
    d = w["w_in"].shape[1]
    gw = jnp.zeros((LANES, 2 * GLA_K), F32)
    gw = gw.at[:GLA_GATE_RANK, :GLA_K].set(w["gla_gw_f"][l])
    gw = gw.at[GLA_GATE_RANK:2 * GLA_GATE_RANK, GLA_K:].set(w["gla_gw_b"][l])
    row = lambda t: t.reshape(1, -1).astype(F32)
    return {
        "g1": row(w["norm1_g"][l]),
        "w_in": jnp.pad(w["w_in"][l], ((0, 0), (0, IN_WIDTH_PAD - IN_WIDTH))).astype(BF16),
        "ones_na": _block_ones(NA_WIDTH, NA_HEAD_DIM),
        "gq": row(jnp.tile(w["na_q_g"][l], NA_HEADS) * (NA_HEAD_DIM ** -0.5 * LOG2E)),
        "gk": row(jnp.tile(w["na_k_g"][l], NA_HEADS)),
        "gw": gw.astype(BF16),
        "gb": row(jnp.concatenate([w["gla_gb_f"][l], w["gla_gb_b"][l]])),
        "conv_w": w["conv_w"][l].astype(F32),
        "conv_b": row(w["conv_b"][l]),
        "conv_ln_g": row(w["conv_ln_g"][l]),
        "conv_ln_b": row(w["conv_ln_b"][l]),
        "conv_pw_w": w["conv_pw_w"][l].astype(BF16),
        "conv_pw_b": row(w["conv_pw_b"][l]),
        "gg": row(jnp.tile(w["gla_out_g"][l], GLA_HEADS)),
        "ones_gla": _block_ones(GLA_V, GLA_DV),
        "w_out": w["w_out"][l].astype(BF16),
        "g2": row(w["norm2_g"][l]),
        "w1": w["w_mlp_in"][l].astype(BF16),
        "w2": w["w_mlp_out"][l].astype(BF16),
    }


def kernel(x, c, ctx, c_ctx, w_ada, b_ada, norm1_g, w_in, na_q_g, na_k_g, na_rpb, conv_w, conv_b, conv_ln_g,
           conv_ln_b, conv_pw_w, conv_pw_b, gla_gw_f, gla_gb_f, gla_gw_b, gla_gb_b, gla_out_g, w_out, norm2_g,
           w_mlp_in, w_mlp_out):
    w = dict(norm1_g=norm1_g, w_in=w_in, na_q_g=na_q_g, na_k_g=na_k_g, conv_w=conv_w, conv_b=conv_b,
             conv_ln_g=conv_ln_g, conv_ln_b=conv_ln_b, conv_pw_w=conv_pw_w, conv_pw_b=conv_pw_b,
             gla_gw_f=gla_gw_f, gla_gb_f=gla_gb_f, gla_gw_b=gla_gw_b, gla_gb_b=gla_gb_b, gla_out_g=gla_out_g,
             w_out=w_out, norm2_g=norm2_g, w_mlp_in=w_mlp_in, w_mlp_out=w_mlp_out)
    b, n, d = x.shape
    lc = ctx.shape[1]
    depth = w_ada.shape[0]
    n_rows = n // GRID_W
    assert n % GRID_W == 0 and n_rows % NA_TILE_ROWS == 0 and n_rows >= NA_KEY_ROWS + 1
    tm_x = tm_c = tn_x = tn_c = 256
    tm_in = 512 if n % 512 == 0 else tm_x
    assert n % tm_x == 0 and lc % tm_c == 0

    ada_rows = -(-(b + 1) // 8) * 8
    cc = jnp.concatenate([c, c_ctx[None, :], jnp.zeros((ada_rows - b - 1, d), c.dtype)], axis=0).astype(F32)
    mod_all = _ada(cc, w_ada, b_ada)

    rope_tabs = _rope_tables(n)
    bias_tabs = _na_bias_tables(na_rpb, n_rows)
    gla_consts = _gla_constants()
    zero_state = jnp.zeros((b, GLA_V, GLA_K), F32)

    xc = ctx
    for l in range(depth):
        lw = _layer_weights(l, w)
        mod = mod_all[l].reshape(ada_rows, 6, d)
        mod_x = mod[:b]
        mod_c = mod[b:b + 1]
        need_ctx_out = l < depth - 1

        (qac, kac, vac, hcc, qgc, kgc, vgc, src, cfc, tfc, cbc, tbc) = _inproj(xc, mod_c, lw, None, tm_c)
        (qa, ka, va, hc, qg, kg, vg, sr, cf, tf, cb, tb) = _inproj(x, mod_x, lw, rope_tabs, tm_in)

        ofc, obc, s_f, s_b = _gla(qgc, kgc, vgc, cfc, tfc, cbc, tbc, zero_state, zero_state, gla_consts, tn_c)
        of, obw, _, _ = _gla(qg, kg, vg, cf, tf, cb, tb, s_f, s_b, gla_consts, tn_x)

        oa = _na_attn(qa, ka, va, kac, vac, bias_tabs, l, n_rows)
        x = _outmlp(x, oa, hc, of, obw, sr, mod_x, lw, tm_in)

        if need_ctx_out:
            oac = _ctx_attn(qac, kac, vac)
            xc = _outmlp(xc, oac, hcc, ofc, obc, src, mod_c, lw, tm_c)
    return x
```

```python
import functools

import numpy as np
import jax
import jax.numpy as jnp
from jax import lax
from jax.experimental import pallas as pl
from jax.experimental.pallas import tpu as pltpu

F32 = jnp.float32
BF16 = jnp.bfloat16

GRID_W = 64
EPS = 1e-6

NA_HEADS = 8
NA_HEAD_DIM = 64
NA_WIN_R = 8
NA_WIN_C = 16
NA_WIDTH = NA_HEADS * NA_HEAD_DIM

CONV_CH = 256
CONV_WIDTH = 31

GLA_HEADS = 4
GLA_DK = 32
GLA_DV = 64
GLA_GATE_RANK = 16
GLA_TAU = 16.0
GLA_K = GLA_HEADS * GLA_DK
GLA_V = GLA_HEADS * GLA_DV
ROPE_THETA = 10000.0

_C_QA, _C_KA, _C_VA, _C_UB = 0, NA_WIDTH, 2 * NA_WIDTH, 3 * NA_WIDTH
_C_QG = _C_UB + 2 * CONV_CH
_C_KG = _C_QG + GLA_K
_C_VG = _C_KG + GLA_K
_C_RG = _C_VG + GLA_V
_C_Z = _C_RG + GLA_V
IN_WIDTH = _C_Z + 2 * GLA_GATE_RANK
LANES = 128
IN_WIDTH_PAD = -(-IN_WIDTH // LANES) * LANES

GLA_SUB = 16
NA_TILE_ROWS = 4
NA_KEY_ROWS = NA_TILE_ROWS + NA_WIN_R - 1
MASK_NEG = -1e30
LOG2E = 1.4426950408889634
VMEM_LIMIT = 56 * 1024 * 1024


def _params(sem, vmem=VMEM_LIMIT):
    return pltpu.CompilerParams(dimension_semantics=sem, vmem_limit_bytes=vmem)


def _silu(t):
    return t * jax.nn.sigmoid(t)


def _dot(a, b):
    return jnp.dot(a, b, preferred_element_type=F32)


def _dot_nt(a, b):
    return lax.dot_general(a, b, (((1,), (1,)), ((), ())), preferred_element_type=F32)


def _ada_kernel(c_ref, w_ref, b_ref, o_ref):
    s = _silu(c_ref[...]).astype(BF16)
    o_ref[0] = _dot(s, w_ref[0].astype(BF16)) + b_ref[0]


def _ada(cc, w_ada, b_ada):
    depth, d, d6 = w_ada.shape
    rows = cc.shape[0]
    tn = 1536 if d6 % 1536 == 0 else d6
    return pl.pallas_call(
        _ada_kernel,
        grid=(depth, d6 // tn),
        in_specs=[
            pl.BlockSpec((rows, d), lambda l, n: (0, 0)),
            pl.BlockSpec((1, d, tn), lambda l, n: (l, 0, n)),
            pl.BlockSpec((1, 1, tn), lambda l, n: (l, 0, n)),
        ],
        out_specs=pl.BlockSpec((1, rows, tn), lambda l, n: (l, 0, n)),
        out_shape=jax.ShapeDtypeStruct((depth, rows, d6), F32),
        compiler_params=_params(("parallel", "parallel")),
        name="ada",
    )(cc, w_ada, b_ada.reshape(depth, 1, d6))


def _block_scan(t, rowm, reverse):
    n = t.shape[0]
    s = 1
    while s < GLA_SUB:
        if reverse:
            t = t + jnp.where(rowm < GLA_SUB - s, pltpu.roll(t, n - s, 0), 0.0)
        else:
            t = t + jnp.where(rowm >= s, pltpu.roll(t, s, 0), 0.0)
        s *= 2
    return t


def _inproj_kernel(*refs, rope):
    if rope:
        (x_ref, mod_ref, g1_ref, w_ref, ones_ref, gq_ref, gk_ref, gw_ref, gb_ref, cos_ref, sin_ref,
         qa_ref, ka_ref, va_ref, hc_ref, qg_ref, kg_ref, vg_ref, sr_ref, cf_ref, tf_ref, cb_ref, tb_ref) = refs
    else:
        (x_ref, mod_ref, g1_ref, w_ref, ones_ref, gq_ref, gk_ref, gw_ref, gb_ref,
         qa_ref, ka_ref, va_ref, hc_ref, qg_ref, kg_ref, vg_ref, sr_ref, cf_ref, tf_ref, cb_ref, tb_ref) = refs
    x = x_ref[0]
    ms = jnp.mean(x * x, axis=-1, keepdims=True)
    h = x * lax.rsqrt(ms + EPS) * g1_ref[...]
    hb = (h * (1.0 + mod_ref[0, 1:2, :]) + mod_ref[0, 0:1, :]).astype(BF16)

    def seg(lo, hi):
        return _dot(hb, w_ref[:, lo:hi])

    def head_norm(t, g_ref):
        ss = _dot((t * t).astype(BF16), ones_ref[...])
        return (t * lax.rsqrt(ss * (1.0 / NA_HEAD_DIM) + EPS) * g_ref[...]).astype(BF16)

    qa_ref[0] = head_norm(seg(_C_QA, _C_KA), gq_ref)
    ka_ref[0] = head_norm(seg(_C_KA, _C_VA), gk_ref)
    va_ref[0] = seg(_C_VA, _C_UB).astype(BF16)

    ub = seg(_C_UB, _C_QG)
    hc_ref[0] = ub[:, :CONV_CH] * jax.nn.sigmoid(ub[:, CONV_CH:])

    qkg = seg(_C_QG, _C_VG)
    qg = qkg[:, :GLA_K]
    kg = qkg[:, GLA_K:]
    if rope:
        cos = cos_ref[...]
        sin = sin_ref[...]
        first = (lax.broadcasted_iota(jnp.int32, (1, GLA_K), 1) & 15) < 8

        def rot(t):
            partner = jnp.where(first, pltpu.roll(t, GLA_K - 8, 1), pltpu.roll(t, 8, 1))
            return t * cos + partner * sin

        qg = rot(qg)
        kg = rot(kg)
    qg_ref[0] = qg * (GLA_DK ** -0.5)
    kg_ref[0] = kg
    vg_ref[0] = seg(_C_VG, _C_RG)
    sr_ref[0] = _silu(seg(_C_RG, _C_Z))

    z = seg(_C_Z, IN_WIDTH_PAD).astype(BF16)
    a = _dot(z, gw_ref[...]) + gb_ref[...]
    la = (jnp.minimum(a, 0.0) - jnp.log(1.0 + jnp.exp(-jnp.abs(a)))) * (LOG2E / GLA_TAU)
    laf = la[:, :GLA_K]
    lab = la[:, GLA_K:]
    rowm = lax.broadcasted_iota(jnp.int32, laf.shape, 0) & (GLA_SUB - 1)
    pf = _block_scan(laf, rowm, False)
    sf = _block_scan(laf, rowm, True)
    pb = _block_scan(lab, rowm, False)
    sb = _block_scan(lab, rowm, True)
    cf_ref[0] = pf
    tf_ref[0] = pf + sf - laf
    cb_ref[0] = sb
    tb_ref[0] = pb + sb - lab


def _inproj(x, mod, lw, rope_tabs, tm):
    b, n, d = x.shape
    nt = n // tm
    rope = rope_tabs is not None
    mod_map = (lambda bi, ti: (bi, 0, 0)) if mod.shape[0] == b else (lambda bi, ti: (0, 0, 0))
    const = lambda bi, ti: (0, 0)
    tile = lambda bi, ti: (bi, ti, 0)
    in_specs = [
        pl.BlockSpec((1, tm, d), tile),
        pl.BlockSpec((1, 6, d), mod_map),
        pl.BlockSpec((1, d), const),
        pl.BlockSpec((d, IN_WIDTH_PAD), const),
        pl.BlockSpec((NA_WIDTH, NA_WIDTH), const),
        pl.BlockSpec((1, NA_WIDTH), const),
        pl.BlockSpec((1, NA_WIDTH), const),
        pl.BlockSpec((LANES, 2 * GLA_K), const),
        pl.BlockSpec((1, 2 * GLA_K), const),
    ]
    args = [x, mod, lw["g1"], lw["w_in"], lw["ones_na"], lw["gq"], lw["gk"], lw["gw"], lw["gb"]]
    if rope:
        in_specs += [pl.BlockSpec((tm, GLA_K), lambda bi, ti: (ti, 0))] * 2
        args += list(rope_tabs)
    widths = [(NA_WIDTH, BF16)] * 3 + [(CONV_CH, F32), (GLA_K, F32), (GLA_K, F32), (GLA_V, F32), (GLA_V, F32)] + [
        (GLA_K, F32)] * 4
    out_specs = [pl.BlockSpec((1, tm, w), tile) for w, _ in widths]
    out_shape = [jax.ShapeDtypeStruct((b, n, w), dt) for w, dt in widths]
    return pl.pallas_call(
        functools.partial(_inproj_kernel, rope=rope),
        grid=(b, nt),
        in_specs=in_specs,
        out_specs=out_specs,
        out_shape=out_shape,
        compiler_params=_params(("parallel", "parallel")),
        name="inproj_rope" if rope else "inproj",
    )(*args)


def _softmax_pv(s_list, v_list):
    m = s_list[0].max(axis=-1, keepdims=True)
    for s in s_list[1:]:
        m = jnp.maximum(m, s.max(axis=-1, keepdims=True))
    l = 0.0
    o = 0.0
    for s, v in zip(s_list, v_list):
        p = jnp.exp2(s - m)
        l = l + p.sum(axis=-1, keepdims=True)
        o = o + _dot(p.astype(BF16), v)
    return o / l


def _na_kernel(q_ref, k_ref, v_ref, kc_ref, vc_ref, blocks_ref, o_ref, sl_ref, sc_ref, bias_ref, *, n_rows):
    j = pl.program_id(1)

    @pl.when((pl.program_id(0) == 0) & (j == 0))
    def _():
        which = _na_block_choice(n_rows)
        for pat in range(which.shape[0]):
            for ri in range(NA_TILE_ROWS):
                rows = slice(ri * GRID_W, (ri + 1) * GRID_W)
                for first in range(0, NA_KEY_ROWS, LANES // GRID_W):
                    group = [blocks_ref[:, int(m)] for m in which[pat, ri, first:first + LANES // GRID_W]]
                    val = group[0] if len(group) == 1 else jnp.concatenate(group, axis=-1)
                    bias_ref[pat, :, rows, first * GRID_W:first * GRID_W + val.shape[-1]] = val

    w0 = jnp.clip(NA_TILE_ROWS * j - NA_WIN_R // 2, 0, n_rows - NA_KEY_ROWS)
    start = pl.multiple_of(w0 * GRID_W, GRID_W)
    keys = pl.ds(start, NA_KEY_ROWS * GRID_W)
    low = lax.broadcasted_iota(jnp.int32, (1, LANES), 1) < NA_HEAD_DIM
    one = jnp.ones((), BF16)
    pattern = jnp.where(j == 0, 0, jnp.where(j == pl.num_programs(1) - 1, 2, 1))

    def scores(h):
        lanes = slice((h // 2) * LANES, (h // 2 + 1) * LANES)
        qp = q_ref[0, :, lanes]
        qe = jnp.where(low if h % 2 == 0 else jnp.logical_not(low), qp, jnp.zeros_like(qp))
        sl_ref[h % 2] = _dot_nt(qe, k_ref[0, keys, lanes]) + bias_ref[pattern, h]
        sc_ref[h % 2] = _dot_nt(qe, kc_ref[0, :, lanes])

    def attend(h):
        lanes = slice((h // 2) * LANES, (h // 2 + 1) * LANES)
        mine = low if h % 2 == 0 else jnp.logical_not(low)
        s_loc = sl_ref[h % 2]
        s_ctx = sc_ref[h % 2]
        m = jnp.maximum(s_loc.max(axis=-1, keepdims=True), s_ctx.max(axis=-1, keepdims=True))
        v_loc = jnp.where(mine, v_ref[0, keys, lanes], one)
        v_ctx = jnp.where(mine, vc_ref[0, :, lanes], one)
        full = (_dot(jnp.exp2(s_loc - m).astype(BF16), v_loc)
                + _dot(jnp.exp2(s_ctx - m).astype(BF16), v_ctx))
        return full / pltpu.roll(full, NA_HEAD_DIM, 1)

    scores(0)
    for p in range(NA_WIDTH // LANES):
        outs = []
        for e in range(2):
            h = 2 * p + e
            if h + 1 < NA_HEADS:
                scores(h + 1)
            outs.append(attend(h))
        o_ref[0, :, p * LANES:(p + 1) * LANES] = jnp.where(low, outs[0], outs[1]).astype(BF16)


def _na_bias_blocks(rpb):
    rpb = rpb.reshape((-1,) + rpb.shape[2:])
    cq = np.arange(GRID_W)
    cs = np.clip(cq - NA_WIN_C // 2, 0, GRID_W - NA_WIN_C)
    ck = np.arange(GRID_W)[None, :]
    col_valid = (ck >= cs[:, None]) & (ck < cs[:, None] + NA_WIN_C)
    padded = jnp.pad(rpb.astype(F32), ((0, 0), (0, 0), (GRID_W, GRID_W)))
    rows_q = [padded[:, :, GRID_W + NA_WIN_C - 1 - q:GRID_W + NA_WIN_C - 1 - q + GRID_W] for q in range(GRID_W)]
    blocks = jnp.stack(rows_q, axis=2)
    blocks = jnp.where(jnp.asarray(col_valid)[None, None], blocks * LOG2E, MASK_NEG)
    masked_block = jnp.full((rpb.shape[0], 1, GRID_W, GRID_W), MASK_NEG, F32)
    return jnp.concatenate([blocks, masked_block], axis=1)


def _na_block_choice(n_rows):
    n_tiles = n_rows // NA_TILE_ROWS
    which = np.full((3, NA_TILE_ROWS, NA_KEY_ROWS), 2 * NA_WIN_R - 1, np.int64)
    for pat, jt in enumerate((0, 1, n_tiles - 1)):
        rq0 = jt * NA_TILE_ROWS
        w0 = int(np.clip(rq0 - NA_WIN_R // 2, 0, n_rows - NA_KEY_ROWS))
        for ri in range(NA_TILE_ROWS):
            r = rq0 + ri
            r0 = int(np.clip(r - NA_WIN_R // 2, 0, n_rows - NA_WIN_R))
            for rki in range(NA_KEY_ROWS):
                rk = w0 + rki
                if r0 <= rk < r0 + NA_WIN_R:
                    which[pat, ri, rki] = rk - r + NA_WIN_R - 1
    return which


def _na_attn(qa, ka, va, kc, vc, bias_blocks, layer, n_rows):
    b, n, _ = qa.shape
    lc = kc.shape[1]
    n_tiles = n_rows // NA_TILE_ROWS
    tq = NA_TILE_ROWS * GRID_W
    nk = NA_KEY_ROWS * GRID_W

    return pl.pallas_call(
        functools.partial(_na_kernel, n_rows=n_rows),
        grid=(b, n_tiles),
        in_specs=[
            pl.BlockSpec((1, tq, NA_WIDTH), lambda bi, j: (bi, j, 0)),
            pl.BlockSpec((1, n, NA_WIDTH), lambda bi, j: (bi, 0, 0)),
            pl.BlockSpec((1, n, NA_WIDTH), lambda bi, j: (bi, 0, 0)),
            pl.BlockSpec((1, lc, NA_WIDTH), lambda bi, j: (bi, 0, 0)),
            pl.BlockSpec((1, lc, NA_WIDTH), lambda bi, j: (bi, 0, 0)),
            pl.BlockSpec((NA_HEADS,) + bias_blocks.shape[1:], lambda bi, j: (layer, 0, 0, 0),
                         pipeline_mode=pl.Buffered(1)),
        ],
        out_specs=pl.BlockSpec((1, tq, NA_WIDTH), lambda bi, j: (bi, j, 0)),
        out_shape=jax.ShapeDtypeStruct((b, n, NA_WIDTH), BF16),
        scratch_shapes=[pltpu.VMEM((2, tq, nk), F32), pltpu.VMEM((2, tq, lc), F32),
                        pltpu.VMEM((3, NA_HEADS, tq, nk), F32)],
        compiler_params=_params(("arbitrary", "arbitrary")),
        name="na_attn",
    )(qa, ka, va, kc, vc, bias_blocks)


def _ctx_attn_kernel(q_ref, k_ref, v_ref, o_ref):
    low = lax.broadcasted_iota(jnp.int32, (1, LANES), 1) < NA_HEAD_DIM
    for p in range(NA_WIDTH // LANES):
        lanes = slice(p * LANES, (p + 1) * LANES)
        qp = q_ref[0, :, lanes]
        kp = k_ref[0, :, lanes]
        vp = v_ref[0, :, lanes]
        outs = []
        for e in range(2):
            qe = jnp.where(low if e == 0 else jnp.logical_not(low), qp, jnp.zeros_like(qp))
            outs.append(_softmax_pv([_dot_nt(qe, kp)], [vp]))
        o_ref[0, :, lanes] = jnp.where(low, outs[0], outs[1]).astype(BF16)


def _ctx_attn(q, k, v):
    b, lc, _ = q.shape
    spec = pl.BlockSpec((1, lc, NA_WIDTH), lambda bi: (bi, 0, 0))
    return pl.pallas_call(
        _ctx_attn_kernel,
        grid=(b,),
        in_specs=[spec, spec, spec],
        out_specs=spec,
        out_shape=jax.ShapeDtypeStruct((b, lc, NA_WIDTH), BF16),
        compiler_params=_params(("parallel",)),
        name="ctx_attn",
    )(q, k, v)


CONV_HALO = 16


CONV_TILE = 64


def _conv_window(h_ref, t0, n):
    t0 = pl.multiple_of(t0, CONV_TILE)
    before = h_ref[0, pl.ds(pl.multiple_of(jnp.maximum(t0 - CONV_HALO, 0), CONV_HALO), CONV_HALO), :]
    after = h_ref[0, pl.ds(pl.multiple_of(jnp.minimum(t0 + CONV_TILE, n - CONV_HALO), CONV_HALO), CONV_HALO), :]
    before = jnp.where(t0 > 0, before, 0.0)
    after = jnp.where(t0 + CONV_TILE < n, after, 0.0)
    return jnp.concatenate([before, h_ref[0, pl.ds(t0, CONV_TILE), :], after], axis=0)


def _conv_stages(h_ref, t0, tm, w_ref, cb_ref, lg_ref, lb_ref, pw_ref, pb_ref, out_ref, slot):
    shift = CONV_HALO - CONV_WIDTH // 2
    rows = CONV_TILE + 2 * CONV_HALO
    n = h_ref.shape[1]
    stages = []
    for s in range(tm // CONV_TILE):
        state = {}

        def taps(phases, s=s, state=state):
            chunk = _conv_window(h_ref, t0 + s * CONV_TILE, n)
            acc = state.get("acc", jnp.zeros((CONV_TILE, CONV_CH), F32))
            for phase in phases:
                off = (phase + shift) % 8
                base = phase + shift - off
                shifted = pltpu.roll(chunk, rows - off, 0) if off else chunk
                shifted = shifted[base:base + CONV_TILE + 8 * ((CONV_WIDTH - 1) // 8), :]
                for k in range(phase, CONV_WIDTH, 8):
                    acc = acc + shifted[k - phase:k - phase + CONV_TILE, :] * w_ref[k:k + 1, :]
            state["acc"] = acc

        def finish(s=s, state=state):
            hcv = state["acc"] + cb_ref[...]
            mu = jnp.mean(hcv, axis=-1, keepdims=True)
            ctr = hcv - mu
            var = jnp.mean(ctr * ctr, axis=-1, keepdims=True)
            y = _silu(ctr * lax.rsqrt(var + EPS) * lg_ref[...] + lb_ref[...])
            out = (_dot(y.astype(BF16), pw_ref[...]) + pb_ref[...]).astype(BF16)
            out_ref[slot, s * CONV_TILE:(s + 1) * CONV_TILE, :] = out

        stages.append(functools.partial(taps, range(0, 4)))
        stages.append(lambda taps=taps, finish=finish: (taps(range(4, 8)), finish()))
    return stages


GLA_INTRA_ROWS = 64


def _gla_intra_stages(q_ref, k_ref, v_ref, c_ref, sel_ref, rep_ref, o_ref, p_ref, w_ref, sp_ref, slot, r0, rowm,
                       forward):
    sub = GLA_INTRA_ROWS
    per = sub // GLA_SUB
    rows = slice(r0, r0 + sub)

    def products():
        q = q_ref[0, rows, :]
        k = k_ref[0, rows, :]
        cum = c_ref[0, rows, :]

        def source_row(t, s):
            t3 = t.reshape(per, GLA_SUB, t.shape[-1])
            return jnp.broadcast_to(t3[:, s:s + 1, :], t3.shape).reshape(t.shape)

        for s in range(GLA_SUB):
            valid = (rowm >= s) if forward else (rowm <= s)
            p = q * source_row(k, s) * jnp.exp2(jnp.where(valid, cum - source_row(cum, s), MASK_NEG))
            p_ref[slot, :, s * GLA_K:(s + 1) * GLA_K] = p.astype(BF16)

    def weights():
        w_ref[slot] = _dot(p_ref[slot], sel_ref[...]).astype(BF16)

    def spread():
        full = _dot(w_ref[slot], rep_ref[...])
        row_chunk = lax.broadcasted_iota(jnp.int32, full.shape, 0) // GLA_SUB
        lane_chunk = lax.broadcasted_iota(jnp.int32, full.shape, 1) // (GLA_HEADS * GLA_SUB)
        sp_ref[slot] = jnp.where(row_chunk == lane_chunk, full, 0.0).astype(BF16)

    def apply():
        v = v_ref[0, rows, :]
        lane_head = lax.broadcasted_iota(jnp.int32, (GLA_SUB, GLA_V), 1) // GLA_DV
        pieces = []
        for c in range(per):
            chunk = v[c * GLA_SUB:(c + 1) * GLA_SUB, :]
            pieces += [jnp.where(lane_head == h, chunk, 0.0) for h in range(GLA_HEADS)]
        o_ref[0, rows, :] = _dot(sp_ref[slot], jnp.concatenate(pieces, axis=0).astype(BF16))

    return products, weights, spread, apply


def _gla_prepare(q_ref, k_ref, v_ref, c_ref, t_ref, qt_ref, kh_ref, vt_ref):
    cum = c_ref[0]
    qt_ref[...] = (q_ref[0] * jnp.exp2(cum)).astype(BF16)
    kh_ref[...] = (k_ref[0] * jnp.exp2(t_ref[0] - cum)).astype(BF16)
    vt_ref[...] = v_ref[0].T.astype(BF16)


def _gla_updates(ib, vt_ref, kh_ref, u_ref, chunk_in_group):
    per_pass = GLA_INTRA_ROWS // GLA_SUB
    g0 = ((ib * GLA_INTRA_ROWS) // LANES) * LANES
    first = (ib * per_pass) % (LANES // GLA_SUB)
    vtg = vt_ref[:, g0:g0 + LANES]
    khg = kh_ref[g0:g0 + LANES, :]
    for c in range(per_pass):
        u_ref[ib * per_pass + c] = _dot(jnp.where(chunk_in_group == first + c, vtg, jnp.zeros((), BF16)), khg)


def _gla_recur(i, t_ref, s_ref, u_ref, sbf_ref, same_head):
    state = s_ref[...]
    sbf_ref[i] = jnp.where(same_head, state, 0.0).astype(BF16)
    s_ref[...] = jnp.exp2(t_ref[0, i * GLA_SUB:i * GLA_SUB + 1, :]) * state + u_ref[i]


def _gla_kernel(qf_ref, kf_ref, vf_ref, cf_ref, tf_ref, qb_ref, kb_ref, vb_ref, cb_ref, tb_ref, s0f_ref, s0b_ref,
                sel_ref, rep_ref, of_ref, ob_ref, sfo_ref, sbo_ref, sf_ref, sb_ref, qtf_ref, khf_ref, vtf_ref, uf_ref,
                sbff_ref, qtb_ref, khb_ref, vtb_ref, ub_ref, sbfb_ref, p_ref, w_ref, sp_ref):
    j = pl.program_id(1)
    tn = qf_ref.shape[1]
    n_chunks = tn // GLA_SUB

    @pl.when(j == 0)
    def _():
        sf_ref[...] = s0f_ref[0]
        sb_ref[...] = s0b_ref[0]

    _gla_prepare(qf_ref, kf_ref, vf_ref, cf_ref, tf_ref, qtf_ref, khf_ref, vtf_ref)
    _gla_prepare(qb_ref, kb_ref, vb_ref, cb_ref, tb_ref, qtb_ref, khb_ref, vtb_ref)

    rowm = lax.broadcasted_iota(jnp.int32, (GLA_INTRA_ROWS, GLA_K), 0) & (GLA_SUB - 1)
    chunk_in_group = lax.broadcasted_iota(jnp.int32, (1, LANES), 1) // GLA_SUB

    blocks = []
    for ib in range(tn // GLA_INTRA_ROWS):
        r0 = ib * GLA_INTRA_ROWS
        blocks.append(_gla_intra_stages(qf_ref, kf_ref, vf_ref, cf_ref, sel_ref, rep_ref, of_ref, p_ref, w_ref,
                                        sp_ref, 2 * ib, r0, rowm, True))
        blocks.append(_gla_intra_stages(qb_ref, kb_ref, vb_ref, cb_ref, sel_ref, rep_ref, ob_ref, p_ref, w_ref,
                                        sp_ref, 2 * ib + 1, r0, rowm, False))
    for ib in range(tn // GLA_INTRA_ROWS):
        blocks[2 * ib][0]()
        _gla_updates(ib, vtf_ref, khf_ref, uf_ref, chunk_in_group)
        blocks[2 * ib + 1][0]()
        _gla_updates(ib, vtb_ref, khb_ref, ub_ref, chunk_in_group)
    for stage in range(1, 4):
        for block in blocks:
            block[stage]()

    head_of_row = lax.broadcasted_iota(jnp.int32, (GLA_V, GLA_K), 0) // GLA_DV
    head_of_col = lax.broadcasted_iota(jnp.int32, (GLA_V, GLA_K), 1) // GLA_DK
    same_head = head_of_row == head_of_col

    for i in range(n_chunks):
        for c, t_ref, s_ref, u_ref, sbf_ref, qt_ref, o_ref in (
                (i, tf_ref, sf_ref, uf_ref, sbff_ref, qtf_ref, of_ref),
                (n_chunks - 1 - i, tb_ref, sb_ref, ub_ref, sbfb_ref, qtb_ref, ob_ref)):
            _gla_recur(c, t_ref, s_ref, u_ref, sbf_ref, same_head)
            rows = slice(c * GLA_SUB, (c + 1) * GLA_SUB)
            o_ref[0, rows, :] = o_ref[0, rows, :] + _dot_nt(qt_ref[rows, :], sbf_ref[c])

    @pl.when(j == pl.num_programs(1) - 1)
    def _():
        sfo_ref[0] = sf_ref[...]
        sbo_ref[0] = sb_ref[...]


def _gla_constants():
    head_of_chan = np.arange(GLA_K) // GLA_DK
    lane = np.arange(GLA_HEADS * GLA_SUB)
    sel = np.concatenate([(head_of_chan[:, None] == lane[None, :] // GLA_SUB) & (lane[None, :] % GLA_SUB == s)
                          for s in range(GLA_SUB)])
    per = GLA_INTRA_ROWS // GLA_SUB
    rep = lane[:, None] == (np.arange(per * lane.size) % lane.size)[None, :]
    return jnp.asarray(sel, BF16), jnp.asarray(rep, BF16)


def _gla(qg, kg, vg, cf, tf, cb, tb, s0f, s0b, consts, tn):
    b, n, _ = qg.shape
    sel, rep = consts
    nt = n // tn
    fwd = lambda bi, j: (bi, j, 0)
    bwd = lambda bi, j: (bi, nt - 1 - j, 0)
    per_b = lambda bi, j: (bi, 0, 0)

    def spec(w, m):
        return pl.BlockSpec((1, tn, w), m)

    state_spec = pl.BlockSpec((1, GLA_V, GLA_K), per_b)
    return pl.pallas_call(
        _gla_kernel,
        grid=(b, nt),
        in_specs=[spec(GLA_K, fwd), spec(GLA_K, fwd), spec(GLA_V, fwd), spec(GLA_K, fwd), spec(GLA_K, fwd),
                  spec(GLA_K, bwd), spec(GLA_K, bwd), spec(GLA_V, bwd), spec(GLA_K, bwd), spec(GLA_K, bwd),
                  state_spec, state_spec,
                  pl.BlockSpec(sel.shape, lambda bi, j: (0, 0)),
                  pl.BlockSpec(rep.shape, lambda bi, j: (0, 0))],
        out_specs=[spec(GLA_V, fwd), spec(GLA_V, bwd), state_spec, state_spec],
        out_shape=[jax.ShapeDtypeStruct((b, n, GLA_V), F32), jax.ShapeDtypeStruct((b, n, GLA_V), F32),
                   jax.ShapeDtypeStruct((b, GLA_V, GLA_K), F32), jax.ShapeDtypeStruct((b, GLA_V, GLA_K), F32)],
        scratch_shapes=[pltpu.VMEM((GLA_V, GLA_K), F32), pltpu.VMEM((GLA_V, GLA_K), F32)] + 2 * [
            pltpu.VMEM((tn, GLA_K), BF16), pltpu.VMEM((tn, GLA_K), BF16), pltpu.VMEM((GLA_V, tn), BF16),
            pltpu.VMEM((tn // GLA_SUB, GLA_V, GLA_K), F32), pltpu.VMEM((tn // GLA_SUB, GLA_V, GLA_K), BF16)] + [
            pltpu.VMEM((2 * tn // GLA_INTRA_ROWS, GLA_INTRA_ROWS, GLA_SUB * GLA_K), BF16),
            pltpu.VMEM((2 * tn // GLA_INTRA_ROWS, GLA_INTRA_ROWS, GLA_HEADS * GLA_SUB), BF16),
            pltpu.VMEM((2 * tn // GLA_INTRA_ROWS, GLA_INTRA_ROWS, GLA_V), BF16)],
        compiler_params=_params(("parallel", "arbitrary")),
        name="gla",
    )(qg, kg, vg, cf, tf, qg, kg, vg, cb, tb, s0f, s0b, sel, rep)


def _outmlp_kernel(x_ref, oa_ref, hc_ref, of_ref, obw_ref, sr_ref, mod_ref, gg_ref, ones_ref, wo_ref, g2_ref,
                   w1_ref, w2_ref, cw_ref, cb_ref, lg_ref, lb_ref, pw_ref, pb_ref, o_ref, ob_ref, *, tf):
    x = x_ref[0]
    tm = x.shape[0]
    ti = pl.program_id(1)

    def conv_stages(i, slot):
        return _conv_stages(hc_ref, i * tm, tm, cw_ref, cb_ref, lg_ref, lb_ref, pw_ref, pb_ref, ob_ref, slot)

    @pl.when(ti == 0)
    def _():
        for stage in conv_stages(ti, 0):
            stage()

    single_tile = hc_ref.shape[1] == tm
    next_stages = [] if single_tile else conv_stages(jnp.minimum(ti + 1, pl.num_programs(1) - 1), (ti + 1) % 2)
    ob = ob_ref[ti % 2]
    og = of_ref[0] + obw_ref[0]
    ss = _dot((og * og).astype(BF16), ones_ref[...])
    oc = (og * lax.rsqrt(ss * (1.0 / GLA_DV) + EPS) * gg_ref[...]) * sr_ref[0]
    c_b = NA_WIDTH
    c_c = NA_WIDTH + CONV_CH
    y = _dot(oa_ref[0], wo_ref[0:c_b, :]) + _dot(ob, wo_ref[c_b:c_c, :]) + _dot(oc.astype(BF16), wo_ref[c_c:, :])
    x1 = x + mod_ref[0, 2:3, :] * y
    ms = jnp.mean(x1 * x1, axis=-1, keepdims=True)
    h2 = x1 * lax.rsqrt(ms + EPS) * g2_ref[...]
    h2 = (h2 * (1.0 + mod_ref[0, 4:5, :]) + mod_ref[0, 3:4, :]).astype(BF16)
    acc = jnp.zeros(x.shape, F32)
    n_ff = w1_ref.shape[1] // tf
    slots = 2 * n_ff

    def run_stages(slot):
        for stage in next_stages[slot * len(next_stages) // slots:(slot + 1) * len(next_stages) // slots]:
            stage()

    for jf in range(n_ff):
        a = jnp.maximum(_dot(h2, w1_ref[:, jf * tf:(jf + 1) * tf]), 0.0)
        run_stages(2 * jf)
        acc = acc + _dot((a * a).astype(BF16), w2_ref[jf * tf:(jf + 1) * tf, :])
        run_stages(2 * jf + 1)
    o_ref[0] = x1 + mod_ref[0, 5:6, :] * acc


def _outmlp(x, oa, hc, of, obw, sr, mod, lw, tm):
    b, n, d = x.shape
    assert tm % CONV_TILE == 0 and n >= CONV_HALO
    vec = pl.BlockSpec((1, CONV_CH), lambda bi, ti: (0, 0))
    dff = lw["w1"].shape[1]
    tf = min(dff, 1024)
    mod_map = (lambda bi, ti: (bi, 0, 0)) if mod.shape[0] == b else (lambda bi, ti: (0, 0, 0))
    const = lambda bi, ti: (0, 0)
    tile = lambda bi, ti: (bi, ti, 0)
    once = pl.Buffered(1)
    return pl.pallas_call(
        functools.partial(_outmlp_kernel, tf=tf),
        grid=(b, n // tm),
        in_specs=[
            pl.BlockSpec((1, tm, d), tile),
            pl.BlockSpec((1, tm, NA_WIDTH), tile),
            pl.BlockSpec((1, n, CONV_CH), lambda bi, ti: (bi, 0, 0)),
            pl.BlockSpec((1, tm, GLA_V), tile),
            pl.BlockSpec((1, tm, GLA_V), tile),
            pl.BlockSpec((1, tm, GLA_V), tile),
            pl.BlockSpec((1, 6, d), mod_map),
            pl.BlockSpec((1, GLA_V), const),
            pl.BlockSpec((GLA_V, GLA_V), const),
            pl.BlockSpec((NA_WIDTH + CONV_CH + GLA_V, d), const, pipeline_mode=once),
            pl.BlockSpec((1, d), const),
            pl.BlockSpec((d, dff), const, pipeline_mode=once),
            pl.BlockSpec((dff, d), const, pipeline_mode=once),
            pl.BlockSpec((CONV_WIDTH, CONV_CH), const),
            vec, vec, vec,
            pl.BlockSpec((CONV_CH, CONV_CH), const),
            vec,
        ],
        out_specs=pl.BlockSpec((1, tm, d), tile),
        out_shape=jax.ShapeDtypeStruct((b, n, d), F32),
        scratch_shapes=[pltpu.VMEM((2, tm, CONV_CH), BF16)],
        compiler_params=_params(("parallel", "arbitrary")),
        name="outmlp",
    )(x, oa, hc, of, obw, sr, mod, lw["gg"], lw["ones_gla"], lw["w_out"], lw["g2"], lw["w1"], lw["w2"],
      lw["conv_w"], lw["conv_b"], lw["conv_ln_g"], lw["conv_ln_b"], lw["conv_pw_w"], lw["conv_pw_b"])


def _block_ones(n, group):
    idx = np.arange(n) // group
    return jnp.asarray(idx[:, None] == idx[None, :], BF16)


def _rope_tables(n):
    half = GLA_DK // 2
    n_freq = half // 2
    lane = np.arange(GLA_K) % GLA_DK
    use_col = lane >= half
    freq_idx = lane % n_freq
    second = (lane % half) >= n_freq
    pos = jnp.arange(n)
    inv_freq = ROPE_THETA ** (-jnp.arange(n_freq, dtype=F32) / n_freq)
    p = jnp.where(jnp.asarray(use_col)[None, :], (pos % GRID_W)[:, None], (pos // GRID_W)[:, None]).astype(F32)
    ang = p * inv_freq[jnp.asarray(freq_idx)][None, :]
    sign = jnp.where(jnp.asarray(second), 1.0, -1.0).astype(F32)
    return jnp.cos(ang), jnp.sin(ang) * sign[None, :]


def _layer_weights(l, w):
    d = w["w_in"].shape[1]
    gw = jnp.zeros((LANES, 2 * GLA_K), F32)
    gw = gw.at[:GLA_GATE_RANK, :GLA_K].set(w["gla_gw_f"][l])
    gw = gw.at[GLA_GATE_RANK:2 * GLA_GATE_RANK, GLA_K:].set(w["gla_gw_b"][l])
    row = lambda t: t.reshape(1, -1).astype(F32)
    return {
        "g1": row(w["norm1_g"][l]),
        "w_in": jnp.pad(w["w_in"][l], ((0, 0), (0, IN_WIDTH_PAD - IN_WIDTH))).astype(BF16),
        "ones_na": _block_ones(NA_WIDTH, NA_HEAD_DIM),
        "gq": row(jnp.tile(w["na_q_g"][l], NA_HEADS) * (NA_HEAD_DIM ** -0.5 * LOG2E)),
        "gk": row(jnp.tile(w["na_k_g"][l], NA_HEADS)),
        "gw": gw.astype(BF16),
        "gb": row(jnp.concatenate([w["gla_gb_f"][l], w["gla_gb_b"][l]])),
        "conv_w": w["conv_w"][l].astype(F32),
        "conv_b": row(w["conv_b"][l]),
        "conv_ln_g": row(w["conv_ln_g"][l]),
        "conv_ln_b": row(w["conv_ln_b"][l]),
        "conv_pw_w": w["conv_pw_w"][l].astype(BF16),
        "conv_pw_b": row(w["conv_pw_b"][l]),
        "gg": row(jnp.tile(w["gla_out_g"][l], GLA_HEADS)),
        "ones_gla": _block_ones(GLA_V, GLA_DV),
        "w_out": w["w_out"][l].astype(BF16),
        "g2": row(w["norm2_g"][l]),
        "w1": w["w_mlp_in"][l].astype(BF16),
        "w2": w["w_mlp_out"][l].astype(BF16),
    }


def kernel(x, c, ctx, c_ctx, w_ada, b_ada, norm1_g, w_in, na_q_g, na_k_g, na_rpb, conv_w, conv_b, conv_ln_g,
           conv_ln_b, conv_pw_w, conv_pw_b, gla_gw_f, gla_gb_f, gla_gw_b, gla_gb_b, gla_out_g, w_out, norm2_g,
           w_mlp_in, w_mlp_out):
    w = dict(norm1_g=norm1_g, w_in=w_in, na_q_g=na_q_g, na_k_g=na_k_g, conv_w=conv_w, conv_b=conv_b,
             conv_ln_g=conv_ln_g, conv_ln_b=conv_ln_b, conv_pw_w=conv_pw_w, conv_pw_b=conv_pw_b,
             gla_gw_f=gla_gw_f, gla_gb_f=gla_gb_f, gla_gw_b=gla_gw_b, gla_gb_b=gla_gb_b, gla_out_g=gla_out_g,
             w_out=w_out, norm2_g=norm2_g, w_mlp_in=w_mlp_in, w_mlp_out=w_mlp_out)
    b, n, d = x.shape
    lc = ctx.shape[1]
    depth = w_ada.shape[0]
    n_rows = n // GRID_W
    assert n % GRID_W == 0 and n_rows % NA_TILE_ROWS == 0 and n_rows >= NA_KEY_ROWS + 1
    tm_x = tm_c = tn_c = 256
    tn_x = 512 if n % 512 == 0 else 256
    tm_in = 512 if n % 512 == 0 else tm_x
    assert n % tm_x == 0 and lc % tm_c == 0

    ada_rows = -(-(b + 1) // 8) * 8
    cc = jnp.concatenate([c, c_ctx[None, :], jnp.zeros((ada_rows - b - 1, d), c.dtype)], axis=0).astype(F32)
    mod_all = _ada(cc, w_ada, b_ada)

    rope_tabs = _rope_tables(n)
    bias_blocks = _na_bias_blocks(na_rpb)
    gla_consts = _gla_constants()
    zero_state = jnp.zeros((b, GLA_V, GLA_K), F32)

    xc = ctx
    for l in range(depth):
        lw = _layer_weights(l, w)
        mod = mod_all[l].reshape(ada_rows, 6, d)
        mod_x = mod[:b]
        mod_c = mod[b:b + 1]
        need_ctx_out = l < depth - 1

        (qac, kac, vac, hcc, qgc, kgc, vgc, src, cfc, tfc, cbc, tbc) = _inproj(xc, mod_c, lw, None, tm_c)
        (qa, ka, va, hc, qg, kg, vg, sr, cf, tf, cb, tb) = _inproj(x, mod_x, lw, rope_tabs, tm_in)

        ofc, obc, s_f, s_b = _gla(qgc, kgc, vgc, cfc, tfc, cbc, tbc, zero_state, zero_state, gla_consts, tn_c)
        of, obw, _, _ = _gla(qg, kg, vg, cf, tf, cb, tb, s_f, s_b, gla_consts, tn_x)

        oa = _na_attn(qa, ka, va, kac, vac, bias_blocks, l, n_rows)
        x = _outmlp(x, oa, hc, of, obw, sr, mod_x, lw, tm_in)

        if need_ctx_out:
            oac = _ctx_attn(qac, kac, vac)
            xc = _outmlp(xc, oac, hcc, ofc, obc, src, mod_c, lw, tm_c)
    return x
```

```python
import functools

import numpy as np
import jax
import jax.numpy as jnp
from jax import lax
from jax.experimental import pallas as pl
from jax.experimental.pallas import tpu as pltpu

F32 = jnp.float32
BF16 = jnp.bfloat16

GRID_W = 64
EPS = 1e-6

NA_HEADS = 8
NA_HEAD_DIM = 64
NA_WIN_R = 8
NA_WIN_C = 16
NA_WIDTH = NA_HEADS * NA_HEAD_DIM

CONV_CH = 256
CONV_WIDTH = 31

GLA_HEADS = 4
GLA_DK = 32
GLA_DV = 64
GLA_GATE_RANK = 16
GLA_TAU = 16.0
GLA_K = GLA_HEADS * GLA_DK
GLA_V = GLA_HEADS * GLA_DV
ROPE_THETA = 10000.0

_C_QA, _C_KA, _C_VA, _C_UB = 0, NA_WIDTH, 2 * NA_WIDTH, 3 * NA_WIDTH
_C_QG = _C_UB + 2 * CONV_CH
_C_KG = _C_QG + GLA_K
_C_VG = _C_KG + GLA_K
_C_RG = _C_VG + GLA_V
_C_Z = _C_RG + GLA_V
IN_WIDTH = _C_Z + 2 * GLA_GATE_RANK
LANES = 128
IN_WIDTH_PAD = -(-IN_WIDTH // LANES) * LANES

GLA_SUB = 16
NA_TILE_ROWS = 4
NA_KEY_ROWS = NA_TILE_ROWS + NA_WIN_R - 1
MASK_NEG = -1e30
LOG2E = 1.4426950408889634
VMEM_LIMIT = 56 * 1024 * 1024


def _params(sem, vmem=VMEM_LIMIT):
    return pltpu.CompilerParams(dimension_semantics=sem, vmem_limit_bytes=vmem)


def _silu(t):
    return t * jax.nn.sigmoid(t)


def _dot(a, b):
    return jnp.dot(a, b, preferred_element_type=F32)


def _dot_nt(a, b):
    return lax.dot_general(a, b, (((1,), (1,)), ((), ())), preferred_element_type=F32)


def _ada_kernel(c_ref, w_ref, b_ref, o_ref):
    s = _silu(c_ref[...]).astype(BF16)
    o_ref[0] = _dot(s, w_ref[0].astype(BF16)) + b_ref[0]


def _ada(cc, w_ada, b_ada):
    depth, d, d6 = w_ada.shape
    rows = cc.shape[0]
    tn = 1536 if d6 % 1536 == 0 else d6
    return pl.pallas_call(
        _ada_kernel,
        grid=(depth, d6 // tn),
        in_specs=[
            pl.BlockSpec((rows, d), lambda l, n: (0, 0)),
            pl.BlockSpec((1, d, tn), lambda l, n: (l, 0, n)),
            pl.BlockSpec((1, 1, tn), lambda l, n: (l, 0, n)),
        ],
        out_specs=pl.BlockSpec((1, rows, tn), lambda l, n: (l, 0, n)),
        out_shape=jax.ShapeDtypeStruct((depth, rows, d6), F32),
        compiler_params=_params(("parallel", "parallel")),
        name="ada",
    )(cc, w_ada, b_ada.reshape(depth, 1, d6))


def _block_scan(t, rowm, reverse):
    n = t.shape[0]
    s = 1
    while s < GLA_SUB:
        if reverse:
            t = t + jnp.where(rowm < GLA_SUB - s, pltpu.roll(t, n - s, 0), 0.0)
        else:
            t = t + jnp.where(rowm >= s, pltpu.roll(t, s, 0), 0.0)
        s *= 2
    return t


def _inproj_kernel(*refs, rope):
    if rope:
        (x_ref, mod_ref, g1_ref, w_ref, ones_ref, gq_ref, gk_ref, gw_ref, gb_ref, cos_ref, sin_ref,
         qa_ref, ka_ref, va_ref, hc_ref, qg_ref, kg_ref, vg_ref, sr_ref, cf_ref, tf_ref, cb_ref, tb_ref) = refs
    else:
        (x_ref, mod_ref, g1_ref, w_ref, ones_ref, gq_ref, gk_ref, gw_ref, gb_ref,
         qa_ref, ka_ref, va_ref, hc_ref, qg_ref, kg_ref, vg_ref, sr_ref, cf_ref, tf_ref, cb_ref, tb_ref) = refs
    x = x_ref[0]
    ms = jnp.mean(x * x, axis=-1, keepdims=True)
    h = x * lax.rsqrt(ms + EPS) * g1_ref[...]
    hb = (h * (1.0 + mod_ref[0, 1:2, :]) + mod_ref[0, 0:1, :]).astype(BF16)

    def seg(lo, hi):
        return _dot(hb, w_ref[:, lo:hi])

    def head_norm(t, g_ref):
        ss = _dot((t * t).astype(BF16), ones_ref[...])
        return (t * lax.rsqrt(ss * (1.0 / NA_HEAD_DIM) + EPS) * g_ref[...]).astype(BF16)

    qa_ref[0] = head_norm(seg(_C_QA, _C_KA), gq_ref)
    ka_ref[0] = head_norm(seg(_C_KA, _C_VA), gk_ref)

    va_ref[0] = seg(_C_VA, _C_UB).astype(BF16)

    ub = seg(_C_UB, _C_QG)
    hc_ref[0] = ub[:, :CONV_CH] * jax.nn.sigmoid(ub[:, CONV_CH:])

    qkg = seg(_C_QG, _C_VG)
    qg = qkg[:, :GLA_K]
    kg = qkg[:, GLA_K:]
    if rope:
        cos = cos_ref[...]
        sin = sin_ref[...]
        first = (lax.broadcasted_iota(jnp.int32, (1, GLA_K), 1) & 15) < 8

        def rot(t):
            partner = jnp.where(first, pltpu.roll(t, GLA_K - 8, 1), pltpu.roll(t, 8, 1))
            return t * cos + partner * sin

        qg = rot(qg)
        kg = rot(kg)
    qg_ref[0] = qg * (GLA_DK ** -0.5)
    kg_ref[0] = kg
    vg_ref[0] = seg(_C_VG, _C_RG)
    sr_ref[0] = _silu(seg(_C_RG, _C_Z))

    z = seg(_C_Z, IN_WIDTH_PAD).astype(BF16)
    a = _dot(z, gw_ref[...]) + gb_ref[...]
    la = (jnp.minimum(a, 0.0) - jnp.log(1.0 + jnp.exp(-jnp.abs(a)))) * (LOG2E / GLA_TAU)
    laf = la[:, :GLA_K]
    lab = la[:, GLA_K:]
    rowm = lax.broadcasted_iota(jnp.int32, laf.shape, 0) & (GLA_SUB - 1)
    pf = _block_scan(laf, rowm, False)
    sf = _block_scan(laf, rowm, True)
    pb = _block_scan(lab, rowm, False)
    sb = _block_scan(lab, rowm, True)
    cf_ref[0] = pf
    tf_ref[0] = pf + sf - laf
    cb_ref[0] = sb
    tb_ref[0] = pb + sb - lab


def _inproj(x, mod, lw, rope_tabs, tm):
    b, n, d = x.shape
    nt = n // tm
    rope = rope_tabs is not None
    mod_map = (lambda bi, ti: (bi, 0, 0)) if mod.shape[0] == b else (lambda bi, ti: (0, 0, 0))
    const = lambda bi, ti: (0, 0)
    tile = lambda bi, ti: (bi, ti, 0)
    in_specs = [
        pl.BlockSpec((1, tm, d), tile),
        pl.BlockSpec((1, 6, d), mod_map),
        pl.BlockSpec((1, d), const),
        pl.BlockSpec((d, IN_WIDTH_PAD), const),
        pl.BlockSpec((NA_WIDTH, NA_WIDTH), const),
        pl.BlockSpec((1, NA_WIDTH), const),
        pl.BlockSpec((1, NA_WIDTH), const),
        pl.BlockSpec((LANES, 2 * GLA_K), const),
        pl.BlockSpec((1, 2 * GLA_K), const),
    ]
    args = [x, mod, lw["g1"], lw["w_in"], lw["ones_na"], lw["gq"], lw["gk"], lw["gw"], lw["gb"]]
    if rope:
        in_specs += [pl.BlockSpec((tm, GLA_K), lambda bi, ti: (ti, 0))] * 2
        args += list(rope_tabs)
    widths = [(NA_WIDTH, BF16)] * 3 + [(CONV_CH, F32), (GLA_K, F32), (GLA_K, F32), (GLA_V, F32), (GLA_V, F32)] + [
        (GLA_K, F32)] * 4
    out_specs = [pl.BlockSpec((1, tm, w), tile) for w, _ in widths]
    out_shape = [jax.ShapeDtypeStruct((b, n, w), dt) for w, dt in widths]
    return pl.pallas_call(
        functools.partial(_inproj_kernel, rope=rope),
        grid=(b, nt),
        in_specs=in_specs,
        out_specs=out_specs,
        out_shape=out_shape,
        compiler_params=_params(("parallel", "parallel")),
        name="inproj_rope" if rope else "inproj",
    )(*args)


def _softmax_pv(s_list, v_list):
    m = s_list[0].max(axis=-1, keepdims=True)
    for s in s_list[1:]:
        m = jnp.maximum(m, s.max(axis=-1, keepdims=True))
    l = 0.0
    o = 0.0
    for s, v in zip(s_list, v_list):
        p = jnp.exp2(s - m)
        l = l + p.sum(axis=-1, keepdims=True)
        o = o + _dot(p.astype(BF16), v)
    return o / l


def _na_kernel(q_ref, k_ref, v_ref, kc_ref, vc_ref, blocks_ref, o_ref, sl_ref, sc_ref, bias_ref, *, n_rows):
    j = pl.program_id(1)

    @pl.when((pl.program_id(0) == 0) & (j == 0))
    def _():
        which = _na_block_choice(n_rows)
        for pat in range(which.shape[0]):
            for ri in range(NA_TILE_ROWS):
                rows = slice(ri * GRID_W, (ri + 1) * GRID_W)
                for first in range(0, NA_KEY_ROWS, LANES // GRID_W):
                    group = [blocks_ref[:, int(m)] for m in which[pat, ri, first:first + LANES // GRID_W]]
                    val = group[0] if len(group) == 1 else jnp.concatenate(group, axis=-1)
                    bias_ref[pat, :, rows, first * GRID_W:first * GRID_W + val.shape[-1]] = val

    w0 = jnp.clip(NA_TILE_ROWS * j - NA_WIN_R // 2, 0, n_rows - NA_KEY_ROWS)
    start = pl.multiple_of(w0 * GRID_W, GRID_W)
    keys = pl.ds(start, NA_KEY_ROWS * GRID_W)
    low = lax.broadcasted_iota(jnp.int32, (1, LANES), 1) < NA_HEAD_DIM
    one = jnp.ones((), BF16)
    pattern = jnp.where(j == 0, 0, jnp.where(j == pl.num_programs(1) - 1, 2, 1))

    def scores(h):
        lanes = slice((h // 2) * LANES, (h // 2 + 1) * LANES)
        qp = q_ref[0, :, lanes]
        qe = jnp.where(low if h % 2 == 0 else jnp.logical_not(low), qp, jnp.zeros_like(qp))
        sl_ref[h % 2] = _dot_nt(qe, k_ref[0, keys, lanes]) + bias_ref[pattern, h]
        sc_ref[h % 2] = _dot_nt(qe, kc_ref[0, :, lanes])

    def attend(h):
        lanes = slice((h // 2) * LANES, (h // 2 + 1) * LANES)
        mine = low if h % 2 == 0 else jnp.logical_not(low)
        s_loc = sl_ref[h % 2]
        s_ctx = sc_ref[h % 2]
        m = jnp.maximum(s_loc.max(axis=-1, keepdims=True), s_ctx.max(axis=-1, keepdims=True))
        v_loc = jnp.where(mine, v_ref[0, keys, lanes], one)
        v_ctx = jnp.where(mine, vc_ref[0, :, lanes], one)
        full = (_dot(jnp.exp2(s_loc - m).astype(BF16), v_loc)
                + _dot(jnp.exp2(s_ctx - m).astype(BF16), v_ctx))
        return full / pltpu.roll(full, NA_HEAD_DIM, 1)

    scores(0)
    for p in range(NA_WIDTH // LANES):
        outs = []
        for e in range(2):
            h = 2 * p + e
            if h + 1 < NA_HEADS:
                scores(h + 1)
            outs.append(attend(h))
        o_ref[0, :, p * LANES:(p + 1) * LANES] = jnp.where(low, outs[0], outs[1]).astype(BF16)


def _na_bias_blocks(rpb):
    rpb = rpb.reshape((-1,) + rpb.shape[2:])
    cq = np.arange(GRID_W)
    cs = np.clip(cq - NA_WIN_C // 2, 0, GRID_W - NA_WIN_C)
    ck = np.arange(GRID_W)[None, :]
    col_valid = (ck >= cs[:, None]) & (ck < cs[:, None] + NA_WIN_C)
    padded = jnp.pad(rpb.astype(F32), ((0, 0), (0, 0), (GRID_W, GRID_W)))
    rows_q = [padded[:, :, GRID_W + NA_WIN_C - 1 - q:GRID_W + NA_WIN_C - 1 - q + GRID_W] for q in range(GRID_W)]
    blocks = jnp.stack(rows_q, axis=2)
    blocks = jnp.where(jnp.asarray(col_valid)[None, None], blocks * LOG2E, MASK_NEG)
    masked_block = jnp.full((rpb.shape[0], 1, GRID_W, GRID_W), MASK_NEG, F32)
    return jnp.concatenate([blocks, masked_block], axis=1)


def _na_block_choice(n_rows):
    n_tiles = n_rows // NA_TILE_ROWS
    which = np.full((3, NA_TILE_ROWS, NA_KEY_ROWS), 2 * NA_WIN_R - 1, np.int64)
    for pat, jt in enumerate((0, 1, n_tiles - 1)):
        rq0 = jt * NA_TILE_ROWS
        w0 = int(np.clip(rq0 - NA_WIN_R // 2, 0, n_rows - NA_KEY_ROWS))
        for ri in range(NA_TILE_ROWS):
            r = rq0 + ri
            r0 = int(np.clip(r - NA_WIN_R // 2, 0, n_rows - NA_WIN_R))
            for rki in range(NA_KEY_ROWS):
                rk = w0 + rki
                if r0 <= rk < r0 + NA_WIN_R:
                    which[pat, ri, rki] = rk - r + NA_WIN_R - 1
    return which


def _na_attn(qa, ka, va, kc, vc, bias_blocks, layer, n_rows):
    b, n, _ = qa.shape
    lc = kc.shape[1]
    n_tiles = n_rows // NA_TILE_ROWS
    tq = NA_TILE_ROWS * GRID_W
    nk = NA_KEY_ROWS * GRID_W

    return pl.pallas_call(
        functools.partial(_na_kernel, n_rows=n_rows),
        grid=(b, n_tiles),
        in_specs=[
            pl.BlockSpec((1, tq, NA_WIDTH), lambda bi, j: (bi, j, 0)),
            pl.BlockSpec((1, n, NA_WIDTH), lambda bi, j: (bi, 0, 0)),
            pl.BlockSpec((1, n, NA_WIDTH), lambda bi, j: (bi, 0, 0)),
            pl.BlockSpec((1, lc, NA_WIDTH), lambda bi, j: (bi, 0, 0)),
            pl.BlockSpec((1, lc, NA_WIDTH), lambda bi, j: (bi, 0, 0)),
            pl.BlockSpec((NA_HEADS,) + bias_blocks.shape[1:], lambda bi, j: (layer, 0, 0, 0),
                         pipeline_mode=pl.Buffered(1)),
        ],
        out_specs=pl.BlockSpec((1, tq, NA_WIDTH), lambda bi, j: (bi, j, 0)),
        out_shape=jax.ShapeDtypeStruct((b, n, NA_WIDTH), BF16),
        scratch_shapes=[pltpu.VMEM((2, tq, nk), F32), pltpu.VMEM((2, tq, lc), F32),
                        pltpu.VMEM((3, NA_HEADS, tq, nk), F32)],
        compiler_params=_params(("arbitrary", "arbitrary")),
        name="na_attn",
    )(qa, ka, va, kc, vc, bias_blocks)


def _ctx_attn_kernel(q_ref, k_ref, v_ref, o_ref):
    low = lax.broadcasted_iota(jnp.int32, (1, LANES), 1) < NA_HEAD_DIM
    for p in range(NA_WIDTH // LANES):
        lanes = slice(p * LANES, (p + 1) * LANES)
        qp = q_ref[0, :, lanes]
        kp = k_ref[0, :, lanes]
        vp = v_ref[0, :, lanes]
        outs = []
        for e in range(2):
            qe = jnp.where(low if e == 0 else jnp.logical_not(low), qp, jnp.zeros_like(qp))
            outs.append(_softmax_pv([_dot_nt(qe, kp)], [vp]))
        o_ref[0, :, lanes] = jnp.where(low, outs[0], outs[1]).astype(BF16)


def _ctx_attn(q, k, v):
    b, lc, _ = q.shape
    spec = pl.BlockSpec((1, lc, NA_WIDTH), lambda bi: (bi, 0, 0))
    return pl.pallas_call(
        _ctx_attn_kernel,
        grid=(b,),
        in_specs=[spec, spec, spec],
        out_specs=spec,
        out_shape=jax.ShapeDtypeStruct((b, lc, NA_WIDTH), BF16),
        compiler_params=_params(("parallel",)),
        name="ctx_attn",
    )(q, k, v)


CONV_HALO = 16


CONV_TILE = 64


def _conv_window(h_ref, t0, n):
    t0 = pl.multiple_of(t0, CONV_TILE)
    before = h_ref[0, pl.ds(pl.multiple_of(jnp.maximum(t0 - CONV_HALO, 0), CONV_HALO), CONV_HALO), :]
    after = h_ref[0, pl.ds(pl.multiple_of(jnp.minimum(t0 + CONV_TILE, n - CONV_HALO), CONV_HALO), CONV_HALO), :]
    before = jnp.where(t0 > 0, before, 0.0)
    after = jnp.where(t0 + CONV_TILE < n, after, 0.0)
    return jnp.concatenate([before, h_ref[0, pl.ds(t0, CONV_TILE), :], after], axis=0)


def _conv_stages(h_ref, t0, tm, w_ref, cb_ref, lg_ref, lb_ref, pw_ref, pb_ref, out_ref, slot):
    shift = CONV_HALO - CONV_WIDTH // 2
    rows = CONV_TILE + 2 * CONV_HALO
    n = h_ref.shape[1]
    stages = []
    for s in range(tm // CONV_TILE):
        state = {}

        def taps(phases, s=s, state=state):
            chunk = _conv_window(h_ref, t0 + s * CONV_TILE, n)
            acc = state.get("acc", jnp.zeros((CONV_TILE, CONV_CH), F32))
            for phase in phases:
                off = (phase + shift) % 8
                base = phase + shift - off
                shifted = pltpu.roll(chunk, rows - off, 0) if off else chunk
                shifted = shifted[base:base + CONV_TILE + 8 * ((CONV_WIDTH - 1) // 8), :]
                for k in range(phase, CONV_WIDTH, 8):
                    acc = acc + shifted[k - phase:k - phase + CONV_TILE, :] * w_ref[k:k + 1, :]
            state["acc"] = acc

        def finish(s=s, state=state):
            hcv = state["acc"] + cb_ref[...]
            mu = jnp.mean(hcv, axis=-1, keepdims=True)
            ctr = hcv - mu
            var = jnp.mean(ctr * ctr, axis=-1, keepdims=True)
            y = _silu(ctr * lax.rsqrt(var + EPS) * lg_ref[...] + lb_ref[...])
            out = (_dot(y.astype(BF16), pw_ref[...]) + pb_ref[...]).astype(BF16)
            out_ref[slot, s * CONV_TILE:(s + 1) * CONV_TILE, :] = out

        stages.append(functools.partial(taps, range(0, 4)))
        stages.append(lambda taps=taps, finish=finish: (taps(range(4, 8)), finish()))
    return stages


GLA_INTRA_ROWS = 64


def _gla_intra_stages(q_ref, k_ref, v_ref, c_ref, sel_ref, rep_ref, o_ref, p_ref, w_ref, sp_ref, slot, r0, rowm,
                       forward):
    sub = GLA_INTRA_ROWS
    per = sub // GLA_SUB
    rows = slice(r0, r0 + sub)

    def products():
        q = q_ref[0, rows, :]
        k = k_ref[0, rows, :]
        cum = c_ref[0, rows, :]

        def source_row(t, s):
            t3 = t.reshape(per, GLA_SUB, t.shape[-1])
            return jnp.broadcast_to(t3[:, s:s + 1, :], t3.shape).reshape(t.shape)

        for s in range(GLA_SUB):
            valid = (rowm >= s) if forward else (rowm <= s)
            p = q * source_row(k, s) * jnp.exp2(jnp.where(valid, cum - source_row(cum, s), MASK_NEG))
            p_ref[slot, :, s * GLA_K:(s + 1) * GLA_K] = p.astype(BF16)

    def weights():
        w_ref[slot] = _dot(p_ref[slot], sel_ref[...]).astype(BF16)

    def spread():
        full = _dot(w_ref[slot], rep_ref[...])
        row_chunk = lax.broadcasted_iota(jnp.int32, full.shape, 0) // GLA_SUB
        lane_chunk = lax.broadcasted_iota(jnp.int32, full.shape, 1) // (GLA_HEADS * GLA_SUB)
        sp_ref[slot] = jnp.where(row_chunk == lane_chunk, full, 0.0).astype(BF16)

    def apply():
        v = v_ref[0, rows, :]
        lane_head = lax.broadcasted_iota(jnp.int32, (GLA_SUB, GLA_V), 1) // GLA_DV
        pieces = []
        for c in range(per):
            chunk = v[c * GLA_SUB:(c + 1) * GLA_SUB, :]
            pieces += [jnp.where(lane_head == h, chunk, 0.0) for h in range(GLA_HEADS)]
        o_ref[0, rows, :] = _dot(sp_ref[slot], jnp.concatenate(pieces, axis=0).astype(BF16))

    return products, weights, spread, apply


def _gla_prepare(q_ref, k_ref, v_ref, c_ref, t_ref, qt_ref, kh_ref, vt_ref):
    cum = c_ref[0]
    qt_ref[...] = (q_ref[0] * jnp.exp2(cum)).astype(BF16)
    kh_ref[...] = (k_ref[0] * jnp.exp2(t_ref[0] - cum)).astype(BF16)
    vt_ref[...] = v_ref[0].T.astype(BF16)


def _gla_updates(ib, vt_ref, kh_ref, u_ref, chunk_in_group):
    per_pass = GLA_INTRA_ROWS // GLA_SUB
    g0 = ((ib * GLA_INTRA_ROWS) // LANES) * LANES
    first = (ib * per_pass) % (LANES // GLA_SUB)
    vtg = vt_ref[:, g0:g0 + LANES]
    khg = kh_ref[g0:g0 + LANES, :]
    for c in range(per_pass):
        u_ref[ib * per_pass + c] = _dot(jnp.where(chunk_in_group == first + c, vtg, jnp.zeros((), BF16)), khg)


def _gla_recur(i, t_ref, s_ref, u_ref, sbf_ref, same_head):
    state = s_ref[...]
    sbf_ref[i] = jnp.where(same_head, state, 0.0).astype(BF16)
    s_ref[...] = jnp.exp2(t_ref[0, i * GLA_SUB:i * GLA_SUB + 1, :]) * state + u_ref[i]


def _gla_kernel(qf_ref, kf_ref, vf_ref, cf_ref, tf_ref, qb_ref, kb_ref, vb_ref, cb_ref, tb_ref, s0f_ref, s0b_ref,
                sel_ref, rep_ref, of_ref, ob_ref, sfo_ref, sbo_ref, sf_ref, sb_ref, qtf_ref, khf_ref, vtf_ref, uf_ref,
                sbff_ref, qtb_ref, khb_ref, vtb_ref, ub_ref, sbfb_ref, p_ref, w_ref, sp_ref):
    j = pl.program_id(1)
    tn = qf_ref.shape[1]
    n_chunks = tn // GLA_SUB

    @pl.when(j == 0)
    def _():
        sf_ref[...] = s0f_ref[0]
        sb_ref[...] = s0b_ref[0]

    _gla_prepare(qf_ref, kf_ref, vf_ref, cf_ref, tf_ref, qtf_ref, khf_ref, vtf_ref)
    _gla_prepare(qb_ref, kb_ref, vb_ref, cb_ref, tb_ref, qtb_ref, khb_ref, vtb_ref)

    rowm = lax.broadcasted_iota(jnp.int32, (GLA_INTRA_ROWS, GLA_K), 0) & (GLA_SUB - 1)
    chunk_in_group = lax.broadcasted_iota(jnp.int32, (1, LANES), 1) // GLA_SUB

    blocks = []
    for ib in range(tn // GLA_INTRA_ROWS):
        r0 = ib * GLA_INTRA_ROWS
        blocks.append(_gla_intra_stages(qf_ref, kf_ref, vf_ref, cf_ref, sel_ref, rep_ref, of_ref, p_ref, w_ref,
                                        sp_ref, 2 * ib, r0, rowm, True))
        blocks.append(_gla_intra_stages(qb_ref, kb_ref, vb_ref, cb_ref, sel_ref, rep_ref, ob_ref, p_ref, w_ref,
                                        sp_ref, 2 * ib + 1, r0, rowm, False))
    for ib in range(tn // GLA_INTRA_ROWS):
        blocks[2 * ib][0]()
        _gla_updates(ib, vtf_ref, khf_ref, uf_ref, chunk_in_group)
        blocks[2 * ib + 1][0]()
        _gla_updates(ib, vtb_ref, khb_ref, ub_ref, chunk_in_group)
    for stage in range(1, 4):
        for block in blocks:
            block[stage]()

    head_of_row = lax.broadcasted_iota(jnp.int32, (GLA_V, GLA_K), 0) // GLA_DV
    head_of_col = lax.broadcasted_iota(jnp.int32, (GLA_V, GLA_K), 1) // GLA_DK
    same_head = head_of_row == head_of_col

    for i in range(n_chunks):
        for c, t_ref, s_ref, u_ref, sbf_ref, qt_ref, o_ref in (
                (i, tf_ref, sf_ref, uf_ref, sbff_ref, qtf_ref, of_ref),
                (n_chunks - 1 - i, tb_ref, sb_ref, ub_ref, sbfb_ref, qtb_ref, ob_ref)):
            _gla_recur(c, t_ref, s_ref, u_ref, sbf_ref, same_head)
            rows = slice(c * GLA_SUB, (c + 1) * GLA_SUB)
            o_ref[0, rows, :] = o_ref[0, rows, :] + _dot_nt(qt_ref[rows, :], sbf_ref[c])

    @pl.when(j == pl.num_programs(1) - 1)
    def _():
        sfo_ref[0] = sf_ref[...]
        sbo_ref[0] = sb_ref[...]


def _gla_constants():
    head_of_chan = np.arange(GLA_K) // GLA_DK
    lane = np.arange(GLA_HEADS * GLA_SUB)
    sel = np.concatenate([(head_of_chan[:, None] == lane[None, :] // GLA_SUB) & (lane[None, :] % GLA_SUB == s)
                          for s in range(GLA_SUB)])
    per = GLA_INTRA_ROWS // GLA_SUB
    rep = lane[:, None] == (np.arange(per * lane.size) % lane.size)[None, :]
    pad = LANES - lane.size
    sel = np.pad(sel, ((0, 0), (0, pad)))
    rep = np.pad(rep, ((0, pad), (0, 0)))
    return jnp.asarray(sel, BF16), jnp.asarray(rep, BF16)


def _gla(qg, kg, vg, cf, tf, cb, tb, s0f, s0b, consts, tn):
    b, n, _ = qg.shape
    sel, rep = consts
    nt = n // tn
    fwd = lambda bi, j: (bi, j, 0)
    bwd = lambda bi, j: (bi, nt - 1 - j, 0)
    per_b = lambda bi, j: (bi, 0, 0)

    def spec(w, m):
        return pl.BlockSpec((1, tn, w), m)

    state_spec = pl.BlockSpec((1, GLA_V, GLA_K), per_b)
    return pl.pallas_call(
        _gla_kernel,
        grid=(b, nt),
        in_specs=[spec(GLA_K, fwd), spec(GLA_K, fwd), spec(GLA_V, fwd), spec(GLA_K, fwd), spec(GLA_K, fwd),
                  spec(GLA_K, bwd), spec(GLA_K, bwd), spec(GLA_V, bwd), spec(GLA_K, bwd), spec(GLA_K, bwd),
                  state_spec, state_spec,
                  pl.BlockSpec(sel.shape, lambda bi, j: (0, 0)),
                  pl.BlockSpec(rep.shape, lambda bi, j: (0, 0))],
        out_specs=[spec(GLA_V, fwd), spec(GLA_V, bwd), state_spec, state_spec],
        out_shape=[jax.ShapeDtypeStruct((b, n, GLA_V), F32), jax.ShapeDtypeStruct((b, n, GLA_V), F32),
                   jax.ShapeDtypeStruct((b, GLA_V, GLA_K), F32), jax.ShapeDtypeStruct((b, GLA_V, GLA_K), F32)],
        scratch_shapes=[pltpu.VMEM((GLA_V, GLA_K), F32), pltpu.VMEM((GLA_V, GLA_K), F32)] + 2 * [
            pltpu.VMEM((tn, GLA_K), BF16), pltpu.VMEM((tn, GLA_K), BF16), pltpu.VMEM((GLA_V, tn), BF16),
            pltpu.VMEM((tn // GLA_SUB, GLA_V, GLA_K), F32), pltpu.VMEM((tn // GLA_SUB, GLA_V, GLA_K), BF16)] + [
            pltpu.VMEM((2 * tn // GLA_INTRA_ROWS, GLA_INTRA_ROWS, GLA_SUB * GLA_K), BF16),
            pltpu.VMEM((2 * tn // GLA_INTRA_ROWS, GLA_INTRA_ROWS, LANES), BF16),
            pltpu.VMEM((2 * tn // GLA_INTRA_ROWS, GLA_INTRA_ROWS, GLA_V), BF16)],
        compiler_params=_params(("parallel", "arbitrary")),
        name="gla",
    )(qg, kg, vg, cf, tf, qg, kg, vg, cb, tb, s0f, s0b, sel, rep)


def _outmlp_kernel(x_ref, oa_ref, hc_ref, of_ref, obw_ref, sr_ref, mod_ref, gg_ref, ones_ref, wo_ref, g2_ref,
                   w1_ref, w2_ref, cw_ref, cb_ref, lg_ref, lb_ref, pw_ref, pb_ref, o_ref, ob_ref, *, tf):
    x = x_ref[0]
    tm = x.shape[0]
    ti = pl.program_id(1)

    def conv_stages(i, slot):
        return _conv_stages(hc_ref, i * tm, tm, cw_ref, cb_ref, lg_ref, lb_ref, pw_ref, pb_ref, ob_ref, slot)

    @pl.when(ti == 0)
    def _():
        for stage in conv_stages(ti, 0):
            stage()

    single_tile = hc_ref.shape[1] == tm
    next_stages = [] if single_tile else conv_stages(jnp.minimum(ti + 1, pl.num_programs(1) - 1), (ti + 1) % 2)
    ob = ob_ref[ti % 2]
    og = of_ref[0] + obw_ref[0]
    ss = _dot((og * og).astype(BF16), ones_ref[...])
    oc = (og * lax.rsqrt(ss * (1.0 / GLA_DV) + EPS) * gg_ref[...]) * sr_ref[0]
    c_b = NA_WIDTH
    c_c = NA_WIDTH + CONV_CH
    y = _dot(oa_ref[0], wo_ref[0:c_b, :]) + _dot(ob, wo_ref[c_b:c_c, :]) + _dot(oc.astype(BF16), wo_ref[c_c:, :])
    x1 = x + mod_ref[0, 2:3, :] * y
    ms = jnp.mean(x1 * x1, axis=-1, keepdims=True)
    h2 = x1 * lax.rsqrt(ms + EPS) * g2_ref[...]
    h2 = (h2 * (1.0 + mod_ref[0, 4:5, :]) + mod_ref[0, 3:4, :]).astype(BF16)
    acc = jnp.zeros(x.shape, F32)
    n_ff = w1_ref.shape[1] // tf
    slots = 2 * n_ff

    def run_stages(slot):
        for stage in next_stages[slot * len(next_stages) // slots:(slot + 1) * len(next_stages) // slots]:
            stage()

    for jf in range(n_ff):
        a = jnp.maximum(_dot(h2, w1_ref[:, jf * tf:(jf + 1) * tf]), 0.0)
        run_stages(2 * jf)
        acc = acc + _dot((a * a).astype(BF16), w2_ref[jf * tf:(jf + 1) * tf, :])
        run_stages(2 * jf + 1)
    o_ref[0] = x1 + mod_ref[0, 5:6, :] * acc


def _outmlp(x, oa, hc, of, obw, sr, mod, lw, tm):
    b, n, d = x.shape
    assert tm % CONV_TILE == 0 and n >= CONV_HALO
    vec = pl.BlockSpec((1, CONV_CH), lambda bi, ti: (0, 0))
    dff = lw["w1"].shape[1]
    tf = min(dff, 1024)
    mod_map = (lambda bi, ti: (bi, 0, 0)) if mod.shape[0] == b else (lambda bi, ti: (0, 0, 0))
    const = lambda bi, ti: (0, 0)
    tile = lambda bi, ti: (bi, ti, 0)
    once = pl.Buffered(1)
    return pl.pallas_call(
        functools.partial(_outmlp_kernel, tf=tf),
        grid=(b, n // tm),
        in_specs=[
            pl.BlockSpec((1, tm, d), tile),
            pl.BlockSpec((1, tm, NA_WIDTH), tile),
            pl.BlockSpec((1, n, CONV_CH), lambda bi, ti: (bi, 0, 0)),
            pl.BlockSpec((1, tm, GLA_V), tile),
            pl.BlockSpec((1, tm, GLA_V), tile),
            pl.BlockSpec((1, tm, GLA_V), tile),
            pl.BlockSpec((1, 6, d), mod_map),
            pl.BlockSpec((1, GLA_V), const),
            pl.BlockSpec((GLA_V, GLA_V), const),
            pl.BlockSpec((NA_WIDTH + CONV_CH + GLA_V, d), const, pipeline_mode=once),
            pl.BlockSpec((1, d), const),
            pl.BlockSpec((d, dff), const, pipeline_mode=once),
            pl.BlockSpec((dff, d), const, pipeline_mode=once),
            pl.BlockSpec((CONV_WIDTH, CONV_CH), const),
            vec, vec, vec,
            pl.BlockSpec((CONV_CH, CONV_CH), const),
            vec,
        ],
        out_specs=pl.BlockSpec((1, tm, d), tile),
        out_shape=jax.ShapeDtypeStruct((b, n, d), F32),
        scratch_shapes=[pltpu.VMEM((2, tm, CONV_CH), BF16)],
        compiler_params=_params(("parallel", "arbitrary")),
        name="outmlp",
    )(x, oa, hc, of, obw, sr, mod, lw["gg"], lw["ones_gla"], lw["w_out"], lw["g2"], lw["w1"], lw["w2"],
      lw["conv_w"], lw["conv_b"], lw["conv_ln_g"], lw["conv_ln_b"], lw["conv_pw_w"], lw["conv_pw_b"])


def _block_ones(n, group):
    idx = np.arange(n) // group
    return jnp.asarray(idx[:, None] == idx[None, :], BF16)


def _rope_tables(n):
    half = GLA_DK // 2
    n_freq = half // 2
    lane = np.arange(GLA_K) % GLA_DK
    use_col = lane >= half
    freq_idx = lane % n_freq
    second = (lane % half) >= n_freq
    pos = jnp.arange(n)
    inv_freq = ROPE_THETA ** (-jnp.arange(n_freq, dtype=F32) / n_freq)
    p = jnp.where(jnp.asarray(use_col)[None, :], (pos % GRID_W)[:, None], (pos // GRID_W)[:, None]).astype(F32)
    ang = p * inv_freq[jnp.asarray(freq_idx)][None, :]
    sign = jnp.where(jnp.asarray(second), 1.0, -1.0).astype(F32)
    return jnp.cos(ang), jnp.sin(ang) * sign[None, :]


def _layer_weights(l, w):
    d = w["w_in"].shape[1]
    gw = jnp.zeros((LANES, 2 * GLA_K), F32)
    gw = gw.at[:GLA_GATE_RANK, :GLA_K].set(w["gla_gw_f"][l])
    gw = gw.at[GLA_GATE_RANK:2 * GLA_GATE_RANK, GLA_K:].set(w["gla_gw_b"][l])
    row = lambda t: t.reshape(1, -1).astype(F32)
    return {
        "g1": row(w["norm1_g"][l]),
        "w_in": jnp.pad(w["w_in"][l], ((0, 0), (0, IN_WIDTH_PAD - IN_WIDTH))).astype(BF16),
        "ones_na": _block_ones(NA_WIDTH, NA_HEAD_DIM),
        "gq": row(jnp.tile(w["na_q_g"][l], NA_HEADS) * (NA_HEAD_DIM ** -0.5 * LOG2E)),
        "gk": row(jnp.tile(w["na_k_g"][l], NA_HEADS)),
        "gw": gw.astype(BF16),
        "gb": row(jnp.concatenate([w["gla_gb_f"][l], w["gla_gb_b"][l]])),
        "conv_w": w["conv_w"][l].astype(F32),
        "conv_b": row(w["conv_b"][l]),
        "conv_ln_g": row(w["conv_ln_g"][l]),
        "conv_ln_b": row(w["conv_ln_b"][l]),
        "conv_pw_w": w["conv_pw_w"][l].astype(BF16),
        "conv_pw_b": row(w["conv_pw_b"][l]),
        "gg": row(jnp.tile(w["gla_out_g"][l], GLA_HEADS)),
        "ones_gla": _block_ones(GLA_V, GLA_DV),
        "w_out": w["w_out"][l].astype(BF16),
        "g2": row(w["norm2_g"][l]),
        "w1": w["w_mlp_in"][l].astype(BF16),
        "w2": w["w_mlp_out"][l].astype(BF16),
    }


def kernel(x, c, ctx, c_ctx, w_ada, b_ada, norm1_g, w_in, na_q_g, na_k_g, na_rpb, conv_w, conv_b, conv_ln_g,
           conv_ln_b, conv_pw_w, conv_pw_b, gla_gw_f, gla_gb_f, gla_gw_b, gla_gb_b, gla_out_g, w_out, norm2_g,
           w_mlp_in, w_mlp_out):
    w = dict(norm1_g=norm1_g, w_in=w_in, na_q_g=na_q_g, na_k_g=na_k_g, conv_w=conv_w, conv_b=conv_b,
             conv_ln_g=conv_ln_g, conv_ln_b=conv_ln_b, conv_pw_w=conv_pw_w, conv_pw_b=conv_pw_b,
             gla_gw_f=gla_gw_f, gla_gb_f=gla_gb_f, gla_gw_b=gla_gw_b, gla_gb_b=gla_gb_b, gla_out_g=gla_out_g,
             w_out=w_out, norm2_g=norm2_g, w_mlp_in=w_mlp_in, w_mlp_out=w_mlp_out)
    b, n, d = x.shape
    lc = ctx.shape[1]
    depth = w_ada.shape[0]
    n_rows = n // GRID_W
    assert n % GRID_W == 0 and n_rows % NA_TILE_ROWS == 0 and n_rows >= NA_KEY_ROWS + 1
    tm_x = tm_c = tn_c = 256
    tn_x = 512 if n % 512 == 0 else 256
    tm_in = 512 if n % 512 == 0 else tm_x
    assert n % tm_x == 0 and lc % tm_c == 0

    ada_rows = -(-(b + 1) // 8) * 8
    cc = jnp.concatenate([c, c_ctx[None, :], jnp.zeros((ada_rows - b - 1, d), c.dtype)], axis=0).astype(F32)
    mod_all = _ada(cc, w_ada, b_ada)

    rope_tabs = _rope_tables(n)
    bias_blocks = _na_bias_blocks(na_rpb)
    gla_consts = _gla_constants()
    zero_state = jnp.zeros((b, GLA_V, GLA_K), F32)

    xc = ctx
    for l in range(depth):
        lw = _layer_weights(l, w)
        mod = mod_all[l].reshape(ada_rows, 6, d)
        mod_x = mod[:b]
        mod_c = mod[b:b + 1]
        need_ctx_out = l < depth - 1

        (qac, kac, vac, hcc, qgc, kgc, vgc, src, cfc, tfc, cbc, tbc) = _inproj(xc, mod_c, lw, None, tm_c)
        (qa, ka, va, hc, qg, kg, vg, sr, cf, tf, cb, tb) = _inproj(x, mod_x, lw, rope_tabs, tm_in)

        ofc, obc, s_f, s_b = _gla(qgc, kgc, vgc, cfc, tfc, cbc, tbc, zero_state, zero_state, gla_consts, tn_c)
        of, obw, _, _ = _gla(qg, kg, vg, cf, tf, cb, tb, s_f, s_b, gla_consts, tn_x)

        oa = _na_attn(qa, ka, va, kac, vac, bias_blocks, l, n_rows)
        x = _outmlp(x, oa, hc, of, obw, sr, mod_x, lw, tm_in)

        if need_ctx_out:
            oac = _ctx_attn(qac, kac, vac)
            xc = _outmlp(xc, oac, hcc, ofc, obc, src, mod_c, lw, tm_c)
    return x
```

```python
import functools

import numpy as np
import jax
import jax.numpy as jnp
from jax import lax
from jax.experimental import pallas as pl
from jax.experimental.pallas import tpu as pltpu

F32 = jnp.float32
BF16 = jnp.bfloat16

GRID_W = 64
EPS = 1e-6

NA_HEADS = 8
NA_HEAD_DIM = 64
NA_WIN_R = 8
NA_WIN_C = 16
NA_WIDTH = NA_HEADS * NA_HEAD_DIM

CONV_CH = 256
CONV_WIDTH = 31

GLA_HEADS = 4
GLA_DK = 32
GLA_DV = 64
GLA_GATE_RANK = 16
GLA_TAU = 16.0
GLA_K = GLA_HEADS * GLA_DK
GLA_V = GLA_HEADS * GLA_DV
ROPE_THETA = 10000.0

_C_QA, _C_KA, _C_VA, _C_UB = 0, NA_WIDTH, 2 * NA_WIDTH, 3 * NA_WIDTH
_C_QG = _C_UB + 2 * CONV_CH
_C_KG = _C_QG + GLA_K
_C_VG = _C_KG + GLA_K
_C_RG = _C_VG + GLA_V
_C_Z = _C_RG + GLA_V
IN_WIDTH = _C_Z + 2 * GLA_GATE_RANK
LANES = 128
IN_WIDTH_PAD = -(-IN_WIDTH // LANES) * LANES

GLA_SUB = 16
NA_TILE_ROWS = 4
NA_KEY_ROWS = NA_TILE_ROWS + NA_WIN_R - 1
MASK_NEG = -1e30
LOG2E = 1.4426950408889634
VMEM_LIMIT = 56 * 1024 * 1024


def _params(sem, vmem=VMEM_LIMIT):
    return pltpu.CompilerParams(dimension_semantics=sem, vmem_limit_bytes=vmem)


def _silu(t):
    return t * jax.nn.sigmoid(t)


def _dot(a, b):
    return jnp.dot(a, b, preferred_element_type=F32)


def _dot_nt(a, b):
    return lax.dot_general(a, b, (((1,), (1,)), ((), ())), preferred_element_type=F32)


def _ada_kernel(c_ref, w_ref, b_ref, o_ref):
    s = _silu(c_ref[...]).astype(BF16)
    o_ref[0] = _dot(s, w_ref[0].astype(BF16)) + b_ref[0]


def _ada(cc, w_ada, b_ada):
    depth, d, d6 = w_ada.shape
    rows = cc.shape[0]
    tn = 1536 if d6 % 1536 == 0 else d6
    return pl.pallas_call(
        _ada_kernel,
        grid=(depth, d6 // tn),
        in_specs=[
            pl.BlockSpec((rows, d), lambda l, n: (0, 0)),
            pl.BlockSpec((1, d, tn), lambda l, n: (l, 0, n)),
            pl.BlockSpec((1, 1, tn), lambda l, n: (l, 0, n)),
        ],
        out_specs=pl.BlockSpec((1, rows, tn), lambda l, n: (l, 0, n)),
        out_shape=jax.ShapeDtypeStruct((depth, rows, d6), F32),
        compiler_params=_params(("parallel", "parallel")),
        name="ada",
    )(cc, w_ada, b_ada.reshape(depth, 1, d6))


def _block_scan(t, rowm, reverse):
    n = t.shape[0]
    s = 1
    while s < GLA_SUB:
        if reverse:
            t = t + jnp.where(rowm < GLA_SUB - s, pltpu.roll(t, n - s, 0), 0.0)
        else:
            t = t + jnp.where(rowm >= s, pltpu.roll(t, s, 0), 0.0)
        s *= 2
    return t


def _inproj_kernel(*refs, rope):
    if rope:
        (x_ref, mod_ref, g1_ref, w_ref, ones_ref, gq_ref, gk_ref, gw_ref, gb_ref, cos_ref, sin_ref,
         qa_ref, ka_ref, va_ref, hc_ref, qg_ref, kg_ref, vg_ref, sr_ref, cf_ref, tf_ref, cb_ref, tb_ref) = refs
    else:
        (x_ref, mod_ref, g1_ref, w_ref, ones_ref, gq_ref, gk_ref, gw_ref, gb_ref,
         qa_ref, ka_ref, va_ref, hc_ref, qg_ref, kg_ref, vg_ref, sr_ref, cf_ref, tf_ref, cb_ref, tb_ref) = refs
    x = x_ref[0]
    ms = jnp.mean(x * x, axis=-1, keepdims=True)
    h = x * lax.rsqrt(ms + EPS) * g1_ref[...]
    hb = (h * (1.0 + mod_ref[0, 1:2, :]) + mod_ref[0, 0:1, :]).astype(BF16)

    def seg(lo, hi):
        return _dot(hb, w_ref[:, lo:hi])

    def head_norm(t, g_ref):
        ss = _dot((t * t).astype(BF16), ones_ref[...])
        return (t * lax.rsqrt(ss * (1.0 / NA_HEAD_DIM) + EPS) * g_ref[...]).astype(BF16)

    qa_ref[0] = head_norm(seg(_C_QA, _C_KA), gq_ref)
    ka_ref[0] = head_norm(seg(_C_KA, _C_VA), gk_ref)

    va_ref[0] = seg(_C_VA, _C_UB).astype(BF16)

    ub = seg(_C_UB, _C_QG)
    hc_ref[0] = ub[:, :CONV_CH] * jax.nn.sigmoid(ub[:, CONV_CH:])

    qkg = seg(_C_QG, _C_VG)
    qg = qkg[:, :GLA_K]
    kg = qkg[:, GLA_K:]
    if rope:
        cos = cos_ref[...]
        sin = sin_ref[...]
        first = (lax.broadcasted_iota(jnp.int32, (1, GLA_K), 1) & 15) < 8

        def rot(t):
            partner = jnp.where(first, pltpu.roll(t, GLA_K - 8, 1), pltpu.roll(t, 8, 1))
            return t * cos + partner * sin

        qg = rot(qg)
        kg = rot(kg)
    qg_ref[0] = qg * (GLA_DK ** -0.5)
    kg_ref[0] = kg
    vg_ref[0] = seg(_C_VG, _C_RG)
    sr_ref[0] = _silu(seg(_C_RG, _C_Z))

    z = seg(_C_Z, IN_WIDTH_PAD).astype(BF16)
    a = _dot(z, gw_ref[...]) + gb_ref[...]
    la = (jnp.minimum(a, 0.0) - jnp.log(1.0 + jnp.exp(-jnp.abs(a)))) * (LOG2E / GLA_TAU)
    laf = la[:, :GLA_K]
    lab = la[:, GLA_K:]
    rowm = lax.broadcasted_iota(jnp.int32, laf.shape, 0) & (GLA_SUB - 1)
    pf = _block_scan(laf, rowm, False)
    sf = _block_scan(laf, rowm, True)
    pb = _block_scan(lab, rowm, False)
    sb = _block_scan(lab, rowm, True)
    cf_ref[0] = pf
    tf_ref[0] = pf + sf - laf
    cb_ref[0] = sb
    tb_ref[0] = pb + sb - lab


def _inproj(x, mod, lw, rope_tabs, tm):
    b, n, d = x.shape
    nt = n // tm
    rope = rope_tabs is not None
    mod_map = (lambda bi, ti: (bi, 0, 0)) if mod.shape[0] == b else (lambda bi, ti: (0, 0, 0))
    const = lambda bi, ti: (0, 0)
    tile = lambda bi, ti: (bi, ti, 0)
    in_specs = [
        pl.BlockSpec((1, tm, d), tile),
        pl.BlockSpec((1, 6, d), mod_map),
        pl.BlockSpec((1, d), const),
        pl.BlockSpec((d, IN_WIDTH_PAD), const),
        pl.BlockSpec((NA_WIDTH, NA_WIDTH), const),
        pl.BlockSpec((1, NA_WIDTH), const),
        pl.BlockSpec((1, NA_WIDTH), const),
        pl.BlockSpec((LANES, 2 * GLA_K), const),
        pl.BlockSpec((1, 2 * GLA_K), const),
    ]
    args = [x, mod, lw["g1"], lw["w_in"], lw["ones_na"], lw["gq"], lw["gk"], lw["gw"], lw["gb"]]
    if rope:
        in_specs += [pl.BlockSpec((tm, GLA_K), lambda bi, ti: (ti, 0))] * 2
        args += list(rope_tabs)
    widths = [(NA_WIDTH, BF16)] * 3 + [(CONV_CH, F32), (GLA_K, F32), (GLA_K, F32), (GLA_V, F32), (GLA_V, F32)] + [
        (GLA_K, F32)] * 4
    out_specs = [pl.BlockSpec((1, tm, w), tile) for w, _ in widths]
    out_shape = [jax.ShapeDtypeStruct((b, n, w), dt) for w, dt in widths]
    return pl.pallas_call(
        functools.partial(_inproj_kernel, rope=rope),
        grid=(b, nt),
        in_specs=in_specs,
        out_specs=out_specs,
        out_shape=out_shape,
        compiler_params=_params(("parallel", "parallel")),
        name="inproj_rope" if rope else "inproj",
    )(*args)


def _softmax_pv(s_list, v_list):
    m = s_list[0].max(axis=-1, keepdims=True)
    for s in s_list[1:]:
        m = jnp.maximum(m, s.max(axis=-1, keepdims=True))
    l = 0.0
    o = 0.0
    for s, v in zip(s_list, v_list):
        p = jnp.exp2(s - m)
        l = l + p.sum(axis=-1, keepdims=True)
        o = o + _dot(p.astype(BF16), v)
    return o / l


def _na_kernel(q_ref, k_ref, v_ref, kc_ref, vc_ref, blocks_ref, o_ref, sl_ref, sc_ref, bias_ref, *, n_rows):
    j = pl.program_id(1)

    @pl.when((pl.program_id(0) == 0) & (j == 0))
    def _():
        which = _na_block_choice(n_rows)
        for pat in range(which.shape[0]):
            for ri in range(NA_TILE_ROWS):
                rows = slice(ri * GRID_W, (ri + 1) * GRID_W)
                for first in range(0, NA_KEY_ROWS, LANES // GRID_W):
                    group = [blocks_ref[:, int(m)] for m in which[pat, ri, first:first + LANES // GRID_W]]
                    val = group[0] if len(group) == 1 else jnp.concatenate(group, axis=-1)
                    bias_ref[pat, :, rows, first * GRID_W:first * GRID_W + val.shape[-1]] = val

    w0 = jnp.clip(NA_TILE_ROWS * j - NA_WIN_R // 2, 0, n_rows - NA_KEY_ROWS)
    start = pl.multiple_of(w0 * GRID_W, GRID_W)
    keys = pl.ds(start, NA_KEY_ROWS * GRID_W)
    low = lax.broadcasted_iota(jnp.int32, (1, LANES), 1) < NA_HEAD_DIM
    one = jnp.ones((), BF16)
    pattern = jnp.where(j == 0, 0, jnp.where(j == pl.num_programs(1) - 1, 2, 1))

    def scores(h):
        lanes = slice((h // 2) * LANES, (h // 2 + 1) * LANES)
        qp = q_ref[0, :, lanes]
        qe = jnp.where(low if h % 2 == 0 else jnp.logical_not(low), qp, jnp.zeros_like(qp))
        sl_ref[h % 2] = _dot_nt(qe, k_ref[0, keys, lanes]) + bias_ref[pattern, h]
        sc_ref[h % 2] = _dot_nt(qe, kc_ref[0, :, lanes])

    def attend(h):
        lanes = slice((h // 2) * LANES, (h // 2 + 1) * LANES)
        mine = low if h % 2 == 0 else jnp.logical_not(low)
        s_loc = sl_ref[h % 2]
        s_ctx = sc_ref[h % 2]
        m = jnp.maximum(s_loc.max(axis=-1, keepdims=True), s_ctx.max(axis=-1, keepdims=True))
        v_loc = jnp.where(mine, v_ref[0, keys, lanes], one)
        v_ctx = jnp.where(mine, vc_ref[0, :, lanes], one)
        full = (_dot(jnp.exp2(s_loc - m).astype(BF16), v_loc)
                + _dot(jnp.exp2(s_ctx - m).astype(BF16), v_ctx))
        return full / pltpu.roll(full, NA_HEAD_DIM, 1)

    scores(0)
    for p in range(NA_WIDTH // LANES):
        outs = []
        for e in range(2):
            h = 2 * p + e
            if h + 1 < NA_HEADS:
                scores(h + 1)
            outs.append(attend(h))
        o_ref[0, :, p * LANES:(p + 1) * LANES] = jnp.where(low, outs[0], outs[1]).astype(BF16)


def _na_bias_blocks(rpb):
    rpb = rpb.reshape((-1,) + rpb.shape[2:])
    cq = np.arange(GRID_W)
    cs = np.clip(cq - NA_WIN_C // 2, 0, GRID_W - NA_WIN_C)
    ck = np.arange(GRID_W)[None, :]
    col_valid = (ck >= cs[:, None]) & (ck < cs[:, None] + NA_WIN_C)
    padded = jnp.pad(rpb.astype(F32), ((0, 0), (0, 0), (GRID_W, GRID_W)))
    rows_q = [padded[:, :, GRID_W + NA_WIN_C - 1 - q:GRID_W + NA_WIN_C - 1 - q + GRID_W] for q in range(GRID_W)]
    blocks = jnp.stack(rows_q, axis=2)
    blocks = jnp.where(jnp.asarray(col_valid)[None, None], blocks * LOG2E, MASK_NEG)
    masked_block = jnp.full((rpb.shape[0], 1, GRID_W, GRID_W), MASK_NEG, F32)
    return jnp.concatenate([blocks, masked_block], axis=1)


def _na_block_choice(n_rows):
    n_tiles = n_rows // NA_TILE_ROWS
    which = np.full((3, NA_TILE_ROWS, NA_KEY_ROWS), 2 * NA_WIN_R - 1, np.int64)
    for pat, jt in enumerate((0, 1, n_tiles - 1)):
        rq0 = jt * NA_TILE_ROWS
        w0 = int(np.clip(rq0 - NA_WIN_R // 2, 0, n_rows - NA_KEY_ROWS))
        for ri in range(NA_TILE_ROWS):
            r = rq0 + ri
            r0 = int(np.clip(r - NA_WIN_R // 2, 0, n_rows - NA_WIN_R))
            for rki in range(NA_KEY_ROWS):
                rk = w0 + rki
                if r0 <= rk < r0 + NA_WIN_R:
                    which[pat, ri, rki] = rk - r + NA_WIN_R - 1
    return which


def _na_attn(qa, ka, va, kc, vc, bias_blocks, layer, n_rows):
    b, n, _ = qa.shape
    lc = kc.shape[1]
    n_tiles = n_rows // NA_TILE_ROWS
    tq = NA_TILE_ROWS * GRID_W
    nk = NA_KEY_ROWS * GRID_W

    return pl.pallas_call(
        functools.partial(_na_kernel, n_rows=n_rows),
        grid=(b, n_tiles),
        in_specs=[
            pl.BlockSpec((1, tq, NA_WIDTH), lambda bi, j: (bi, j, 0)),
            pl.BlockSpec((1, n, NA_WIDTH), lambda bi, j: (bi, 0, 0)),
            pl.BlockSpec((1, n, NA_WIDTH), lambda bi, j: (bi, 0, 0)),
            pl.BlockSpec((1, lc, NA_WIDTH), lambda bi, j: (bi, 0, 0)),
            pl.BlockSpec((1, lc, NA_WIDTH), lambda bi, j: (bi, 0, 0)),
            pl.BlockSpec((NA_HEADS,) + bias_blocks.shape[1:], lambda bi, j: (layer, 0, 0, 0),
                         pipeline_mode=pl.Buffered(1)),
        ],
        out_specs=pl.BlockSpec((1, tq, NA_WIDTH), lambda bi, j: (bi, j, 0)),
        out_shape=jax.ShapeDtypeStruct((b, n, NA_WIDTH), BF16),
        scratch_shapes=[pltpu.VMEM((2, tq, nk), F32), pltpu.VMEM((2, tq, lc), F32),
                        pltpu.VMEM((3, NA_HEADS, tq, nk), F32)],
        compiler_params=_params(("arbitrary", "arbitrary")),
        name="na_attn",
    )(qa, ka, va, kc, vc, bias_blocks)


def _ctx_attn_kernel(q_ref, k_ref, v_ref, o_ref):
    low = lax.broadcasted_iota(jnp.int32, (1, LANES), 1) < NA_HEAD_DIM
    for p in range(NA_WIDTH // LANES):
        lanes = slice(p * LANES, (p + 1) * LANES)
        qp = q_ref[0, :, lanes]
        kp = k_ref[0, :, lanes]
        vp = v_ref[0, :, lanes]
        outs = []
        for e in range(2):
            qe = jnp.where(low if e == 0 else jnp.logical_not(low), qp, jnp.zeros_like(qp))
            outs.append(_softmax_pv([_dot_nt(qe, kp)], [vp]))
        o_ref[0, :, lanes] = jnp.where(low, outs[0], outs[1]).astype(BF16)


def _ctx_attn(q, k, v):
    b, lc, _ = q.shape
    spec = pl.BlockSpec((1, lc, NA_WIDTH), lambda bi: (bi, 0, 0))
    return pl.pallas_call(
        _ctx_attn_kernel,
        grid=(b,),
        in_specs=[spec, spec, spec],
        out_specs=spec,
        out_shape=jax.ShapeDtypeStruct((b, lc, NA_WIDTH), BF16),
        compiler_params=_params(("parallel",)),
        name="ctx_attn",
    )(q, k, v)


CONV_HALO = 16


CONV_TILE = 64


def _conv_window(h_ref, t0, n):
    t0 = pl.multiple_of(t0, CONV_TILE)
    before = h_ref[0, pl.ds(pl.multiple_of(jnp.maximum(t0 - CONV_HALO, 0), CONV_HALO), CONV_HALO), :]
    after = h_ref[0, pl.ds(pl.multiple_of(jnp.minimum(t0 + CONV_TILE, n - CONV_HALO), CONV_HALO), CONV_HALO), :]
    before = jnp.where(t0 > 0, before, 0.0)
    after = jnp.where(t0 + CONV_TILE < n, after, 0.0)
    return jnp.concatenate([before, h_ref[0, pl.ds(t0, CONV_TILE), :], after], axis=0)


def _conv_stages(h_ref, t0, tm, w_ref, cb_ref, lg_ref, lb_ref, pw_ref, pb_ref, out_ref, slot):
    shift = CONV_HALO - CONV_WIDTH // 2
    rows = CONV_TILE + 2 * CONV_HALO
    n = h_ref.shape[1]
    stages = []
    for s in range(tm // CONV_TILE):
        state = {}

        def taps(phases, s=s, state=state):
            chunk = _conv_window(h_ref, t0 + s * CONV_TILE, n)
            acc = state.get("acc", jnp.zeros((CONV_TILE, CONV_CH), F32))
            for phase in phases:
                off = (phase + shift) % 8
                base = phase + shift - off
                shifted = pltpu.roll(chunk, rows - off, 0) if off else chunk
                shifted = shifted[base:base + CONV_TILE + 8 * ((CONV_WIDTH - 1) // 8), :]
                for k in range(phase, CONV_WIDTH, 8):
                    acc = acc + shifted[k - phase:k - phase + CONV_TILE, :] * w_ref[k:k + 1, :]
            state["acc"] = acc

        def finish(s=s, state=state):
            hcv = state["acc"] + cb_ref[...]
            mu = jnp.mean(hcv, axis=-1, keepdims=True)
            ctr = hcv - mu
            var = jnp.mean(ctr * ctr, axis=-1, keepdims=True)
            y = _silu(ctr * lax.rsqrt(var + EPS) * lg_ref[...] + lb_ref[...])
            out = (_dot(y.astype(BF16), pw_ref[...]) + pb_ref[...]).astype(BF16)
            out_ref[slot, s * CONV_TILE:(s + 1) * CONV_TILE, :] = out

        stages.append(functools.partial(taps, range(0, 4)))
        stages.append(lambda taps=taps, finish=finish: (taps(range(4, 8)), finish()))
    return stages


GLA_INTRA_ROWS = 64


def _gla_intra_stages(q_ref, k_ref, v_ref, c_ref, sel_ref, rep_ref, o_ref, p_ref, w_ref, sp_ref, slot, r0, rowm,
                       forward):
    sub = GLA_INTRA_ROWS
    per = sub // GLA_SUB
    rows = slice(r0, r0 + sub)

    def products():
        q = q_ref[0, rows, :]
        k = k_ref[0, rows, :]
        cum = c_ref[0, rows, :]

        def source_row(t, s):
            t3 = t.reshape(per, GLA_SUB, t.shape[-1])
            return jnp.broadcast_to(t3[:, s:s + 1, :], t3.shape).reshape(t.shape)

        for s in range(GLA_SUB):
            valid = (rowm >= s) if forward else (rowm <= s)
            p = q * source_row(k, s) * jnp.exp2(jnp.where(valid, cum - source_row(cum, s), MASK_NEG))
            p_ref[slot, :, s * GLA_K:(s + 1) * GLA_K] = p.astype(BF16)

    def weights():
        w_ref[slot] = _dot(p_ref[slot], sel_ref[...]).astype(BF16)

    def spread():
        full = _dot(w_ref[slot], rep_ref[...])
        row_chunk = lax.broadcasted_iota(jnp.int32, full.shape, 0) // GLA_SUB
        lane_chunk = lax.broadcasted_iota(jnp.int32, full.shape, 1) // (GLA_HEADS * GLA_SUB)
        sp_ref[slot] = jnp.where(row_chunk == lane_chunk, full, 0.0).astype(BF16)

    def apply():
        v = v_ref[0, rows, :]
        lane_head = lax.broadcasted_iota(jnp.int32, (GLA_SUB, GLA_V), 1) // GLA_DV
        pieces = []
        for c in range(per):
            chunk = v[c * GLA_SUB:(c + 1) * GLA_SUB, :]
            pieces += [jnp.where(lane_head == h, chunk, 0.0) for h in range(GLA_HEADS)]
        o_ref[0, rows, :] = _dot(sp_ref[slot], jnp.concatenate(pieces, axis=0).astype(BF16))

    return products, weights, spread, apply


def _gla_prepare(q_ref, k_ref, v_ref, c_ref, t_ref, qt_ref, kh_ref, vt_ref):
    cum = c_ref[0]
    qt_ref[...] = (q_ref[0] * jnp.exp2(cum)).astype(BF16)
    kh_ref[...] = (k_ref[0] * jnp.exp2(t_ref[0] - cum)).astype(BF16)
    vt_ref[...] = v_ref[0].T.astype(BF16)


def _gla_updates(ib, vt_ref, kh_ref, u_ref, chunk_in_group):
    per_pass = GLA_INTRA_ROWS // GLA_SUB
    g0 = ((ib * GLA_INTRA_ROWS) // LANES) * LANES
    first = (ib * per_pass) % (LANES // GLA_SUB)
    vtg = vt_ref[:, g0:g0 + LANES]
    khg = kh_ref[g0:g0 + LANES, :]
    for c in range(per_pass):
        u_ref[ib * per_pass + c] = _dot(jnp.where(chunk_in_group == first + c, vtg, jnp.zeros((), BF16)), khg)


def _gla_recur(i, t_ref, state, u_ref, sbf_ref, same_head):
    sbf_ref[i] = jnp.where(same_head, state, 0.0).astype(BF16)
    return jnp.exp2(t_ref[0, i * GLA_SUB:i * GLA_SUB + 1, :]) * state + u_ref[i]


def _gla_kernel(qf_ref, kf_ref, vf_ref, cf_ref, tf_ref, qb_ref, kb_ref, vb_ref, cb_ref, tb_ref, s0f_ref, s0b_ref,
                sel_ref, rep_ref, of_ref, ob_ref, sfo_ref, sbo_ref, sf_ref, sb_ref, qtf_ref, khf_ref, vtf_ref, uf_ref,
                sbff_ref, qtb_ref, khb_ref, vtb_ref, ub_ref, sbfb_ref, p_ref, w_ref, sp_ref):
    j = pl.program_id(1)
    tn = qf_ref.shape[1]
    n_chunks = tn // GLA_SUB

    @pl.when(j == 0)
    def _():
        sf_ref[...] = s0f_ref[0]
        sb_ref[...] = s0b_ref[0]

    _gla_prepare(qf_ref, kf_ref, vf_ref, cf_ref, tf_ref, qtf_ref, khf_ref, vtf_ref)
    _gla_prepare(qb_ref, kb_ref, vb_ref, cb_ref, tb_ref, qtb_ref, khb_ref, vtb_ref)

    rowm = lax.broadcasted_iota(jnp.int32, (GLA_INTRA_ROWS, GLA_K), 0) & (GLA_SUB - 1)
    chunk_in_group = lax.broadcasted_iota(jnp.int32, (1, LANES), 1) // GLA_SUB

    blocks = []
    for ib in range(tn // GLA_INTRA_ROWS):
        r0 = ib * GLA_INTRA_ROWS
        blocks.append(_gla_intra_stages(qf_ref, kf_ref, vf_ref, cf_ref, sel_ref, rep_ref, of_ref, p_ref, w_ref,
                                        sp_ref, 2 * ib, r0, rowm, True))
        blocks.append(_gla_intra_stages(qb_ref, kb_ref, vb_ref, cb_ref, sel_ref, rep_ref, ob_ref, p_ref, w_ref,
                                        sp_ref, 2 * ib + 1, r0, rowm, False))
    for ib in range(tn // GLA_INTRA_ROWS):
        blocks[2 * ib][0]()
        _gla_updates(ib, vtf_ref, khf_ref, uf_ref, chunk_in_group)
        blocks[2 * ib + 1][0]()
        _gla_updates(ib, vtb_ref, khb_ref, ub_ref, chunk_in_group)
    for stage in range(1, 4):
        for block in blocks:
            block[stage]()

    head_of_row = lax.broadcasted_iota(jnp.int32, (GLA_V, GLA_K), 0) // GLA_DV
    head_of_col = lax.broadcasted_iota(jnp.int32, (GLA_V, GLA_K), 1) // GLA_DK
    same_head = head_of_row == head_of_col

    states = [sf_ref[...], sb_ref[...]]
    for i in range(n_chunks):
        for d, (c, t_ref, u_ref, sbf_ref, qt_ref, o_ref) in enumerate((
                (i, tf_ref, uf_ref, sbff_ref, qtf_ref, of_ref),
                (n_chunks - 1 - i, tb_ref, ub_ref, sbfb_ref, qtb_ref, ob_ref))):
            states[d] = _gla_recur(c, t_ref, states[d], u_ref, sbf_ref, same_head)
            rows = slice(c * GLA_SUB, (c + 1) * GLA_SUB)
            o_ref[0, rows, :] = o_ref[0, rows, :] + _dot_nt(qt_ref[rows, :], sbf_ref[c])
    sf_ref[...] = states[0]
    sb_ref[...] = states[1]

    @pl.when(j == pl.num_programs(1) - 1)
    def _():
        sfo_ref[0] = sf_ref[...]
        sbo_ref[0] = sb_ref[...]


def _gla_constants():
    head_of_chan = np.arange(GLA_K) // GLA_DK
    lane = np.arange(GLA_HEADS * GLA_SUB)
    sel = np.concatenate([(head_of_chan[:, None] == lane[None, :] // GLA_SUB) & (lane[None, :] % GLA_SUB == s)
                          for s in range(GLA_SUB)])
    per = GLA_INTRA_ROWS // GLA_SUB
    rep = lane[:, None] == (np.arange(per * lane.size) % lane.size)[None, :]
    pad = LANES - lane.size
    sel = np.pad(sel, ((0, 0), (0, pad)))
    rep = np.pad(rep, ((0, pad), (0, 0)))
    return jnp.asarray(sel, BF16), jnp.asarray(rep, BF16)


def _gla(qg, kg, vg, cf, tf, cb, tb, s0f, s0b, consts, tn):
    b, n, _ = qg.shape
    sel, rep = consts
    nt = n // tn
    fwd = lambda bi, j: (bi, j, 0)
    bwd = lambda bi, j: (bi, nt - 1 - j, 0)
    per_b = lambda bi, j: (bi, 0, 0)

    def spec(w, m):
        return pl.BlockSpec((1, tn, w), m)

    state_spec = pl.BlockSpec((1, GLA_V, GLA_K), per_b)
    return pl.pallas_call(
        _gla_kernel,
        grid=(b, nt),
        in_specs=[spec(GLA_K, fwd), spec(GLA_K, fwd), spec(GLA_V, fwd), spec(GLA_K, fwd), spec(GLA_K, fwd),
                  spec(GLA_K, bwd), spec(GLA_K, bwd), spec(GLA_V, bwd), spec(GLA_K, bwd), spec(GLA_K, bwd),
                  state_spec, state_spec,
                  pl.BlockSpec(sel.shape, lambda bi, j: (0, 0)),
                  pl.BlockSpec(rep.shape, lambda bi, j: (0, 0))],
        out_specs=[spec(GLA_V, fwd), spec(GLA_V, bwd), state_spec, state_spec],
        out_shape=[jax.ShapeDtypeStruct((b, n, GLA_V), F32), jax.ShapeDtypeStruct((b, n, GLA_V), F32),
                   jax.ShapeDtypeStruct((b, GLA_V, GLA_K), F32), jax.ShapeDtypeStruct((b, GLA_V, GLA_K), F32)],
        scratch_shapes=[pltpu.VMEM((GLA_V, GLA_K), F32), pltpu.VMEM((GLA_V, GLA_K), F32)] + 2 * [
            pltpu.VMEM((tn, GLA_K), BF16), pltpu.VMEM((tn, GLA_K), BF16), pltpu.VMEM((GLA_V, tn), BF16),
            pltpu.VMEM((tn // GLA_SUB, GLA_V, GLA_K), F32), pltpu.VMEM((tn // GLA_SUB, GLA_V, GLA_K), BF16)] + [
            pltpu.VMEM((2 * tn // GLA_INTRA_ROWS, GLA_INTRA_ROWS, GLA_SUB * GLA_K), BF16),
            pltpu.VMEM((2 * tn // GLA_INTRA_ROWS, GLA_INTRA_ROWS, LANES), BF16),
            pltpu.VMEM((2 * tn // GLA_INTRA_ROWS, GLA_INTRA_ROWS, GLA_V), BF16)],
        compiler_params=_params(("parallel", "arbitrary")),
        name="gla",
    )(qg, kg, vg, cf, tf, qg, kg, vg, cb, tb, s0f, s0b, sel, rep)


def _outmlp_kernel(x_ref, oa_ref, hc_ref, of_ref, obw_ref, sr_ref, mod_ref, gg_ref, ones_ref, wo_ref, g2_ref,
                   w1_ref, w2_ref, cw_ref, cb_ref, lg_ref, lb_ref, pw_ref, pb_ref, o_ref, ob_ref, *, tf):
    x = x_ref[0]
    tm = x.shape[0]
    ti = pl.program_id(1)

    def conv_stages(i, slot):
        return _conv_stages(hc_ref, i * tm, tm, cw_ref, cb_ref, lg_ref, lb_ref, pw_ref, pb_ref, ob_ref, slot)

    @pl.when(ti == 0)
    def _():
        for stage in conv_stages(ti, 0):
            stage()

    single_tile = hc_ref.shape[1] == tm
    next_stages = [] if single_tile else conv_stages(jnp.minimum(ti + 1, pl.num_programs(1) - 1), (ti + 1) % 2)
    ob = ob_ref[ti % 2]
    og = of_ref[0] + obw_ref[0]
    ss = _dot((og * og).astype(BF16), ones_ref[...])
    oc = (og * lax.rsqrt(ss * (1.0 / GLA_DV) + EPS) * gg_ref[...]) * sr_ref[0]
    c_b = NA_WIDTH
    c_c = NA_WIDTH + CONV_CH
    y = _dot(oa_ref[0], wo_ref[0:c_b, :]) + _dot(ob, wo_ref[c_b:c_c, :]) + _dot(oc.astype(BF16), wo_ref[c_c:, :])
    x1 = x + mod_ref[0, 2:3, :] * y
    ms = jnp.mean(x1 * x1, axis=-1, keepdims=True)
    h2 = x1 * lax.rsqrt(ms + EPS) * g2_ref[...]
    h2 = (h2 * (1.0 + mod_ref[0, 4:5, :]) + mod_ref[0, 3:4, :]).astype(BF16)
    acc = jnp.zeros(x.shape, F32)
    n_ff = w1_ref.shape[1] // tf
    slots = 2 * n_ff

    def run_stages(slot):
        for stage in next_stages[slot * len(next_stages) // slots:(slot + 1) * len(next_stages) // slots]:
            stage()

    for jf in range(n_ff):
        a = jnp.maximum(_dot(h2, w1_ref[:, jf * tf:(jf + 1) * tf]), 0.0)
        run_stages(2 * jf)
        acc = acc + _dot((a * a).astype(BF16), w2_ref[jf * tf:(jf + 1) * tf, :])
        run_stages(2 * jf + 1)
    o_ref[0] = x1 + mod_ref[0, 5:6, :] * acc


def _outmlp(x, oa, hc, of, obw, sr, mod, lw, tm):
    b, n, d = x.shape
    assert tm % CONV_TILE == 0 and n >= CONV_HALO
    vec = pl.BlockSpec((1, CONV_CH), lambda bi, ti: (0, 0))
    dff = lw["w1"].shape[1]
    tf = min(dff, 1024)
    mod_map = (lambda bi, ti: (bi, 0, 0)) if mod.shape[0] == b else (lambda bi, ti: (0, 0, 0))
    const = lambda bi, ti: (0, 0)
    tile = lambda bi, ti: (bi, ti, 0)
    once = pl.Buffered(1)
    return pl.pallas_call(
        functools.partial(_outmlp_kernel, tf=tf),
        grid=(b, n // tm),
        in_specs=[
            pl.BlockSpec((1, tm, d), tile),
            pl.BlockSpec((1, tm, NA_WIDTH), tile),
            pl.BlockSpec((1, n, CONV_CH), lambda bi, ti: (bi, 0, 0)),
            pl.BlockSpec((1, tm, GLA_V), tile),
            pl.BlockSpec((1, tm, GLA_V), tile),
            pl.BlockSpec((1, tm, GLA_V), tile),
            pl.BlockSpec((1, 6, d), mod_map),
            pl.BlockSpec((1, GLA_V), const),
            pl.BlockSpec((GLA_V, GLA_V), const),
            pl.BlockSpec((NA_WIDTH + CONV_CH + GLA_V, d), const, pipeline_mode=once),
            pl.BlockSpec((1, d), const),
            pl.BlockSpec((d, dff), const, pipeline_mode=once),
            pl.BlockSpec((dff, d), const, pipeline_mode=once),
            pl.BlockSpec((CONV_WIDTH, CONV_CH), const),
            vec, vec, vec,
            pl.BlockSpec((CONV_CH, CONV_CH), const),
            vec,
        ],
        out_specs=pl.BlockSpec((1, tm, d), tile),
        out_shape=jax.ShapeDtypeStruct((b, n, d), F32),
        scratch_shapes=[pltpu.VMEM((2, tm, CONV_CH), BF16)],
        compiler_params=_params(("parallel", "arbitrary")),
        name="outmlp",
    )(x, oa, hc, of, obw, sr, mod, lw["gg"], lw["ones_gla"], lw["w_out"], lw["g2"], lw["w1"], lw["w2"],
      lw["conv_w"], lw["conv_b"], lw["conv_ln_g"], lw["conv_ln_b"], lw["conv_pw_w"], lw["conv_pw_b"])


def _block_ones(n, group):
    idx = np.arange(n) // group
    return jnp.asarray(idx[:, None] == idx[None, :], BF16)


def _rope_tables(n):
    half = GLA_DK // 2
    n_freq = half // 2
    lane = np.arange(GLA_K) % GLA_DK
    use_col = lane >= half
    freq_idx = lane % n_freq
    second = (lane % half) >= n_freq
    pos = jnp.arange(n)
    inv_freq = ROPE_THETA ** (-jnp.arange(n_freq, dtype=F32) / n_freq)
    p = jnp.where(jnp.asarray(use_col)[None, :], (pos % GRID_W)[:, None], (pos // GRID_W)[:, None]).astype(F32)
    ang = p * inv_freq[jnp.asarray(freq_idx)][None, :]
    sign = jnp.where(jnp.asarray(second), 1.0, -1.0).astype(F32)
    return jnp.cos(ang), jnp.sin(ang) * sign[None, :]


def _layer_weights(l, w):
    d = w["w_in"].shape[1]
    gw = jnp.zeros((LANES, 2 * GLA_K), F32)
    gw = gw.at[:GLA_GATE_RANK, :GLA_K].set(w["gla_gw_f"][l])
    gw = gw.at[GLA_GATE_RANK:2 * GLA_GATE_RANK, GLA_K:].set(w["gla_gw_b"][l])
    row = lambda t: t.reshape(1, -1).astype(F32)
    return {
        "g1": row(w["norm1_g"][l]),
        "w_in": jnp.pad(w["w_in"][l], ((0, 0), (0, IN_WIDTH_PAD - IN_WIDTH))).astype(BF16),
        "ones_na": _block_ones(NA_WIDTH, NA_HEAD_DIM),
        "gq": row(jnp.tile(w["na_q_g"][l], NA_HEADS) * (NA_HEAD_DIM ** -0.5 * LOG2E)),
        "gk": row(jnp.tile(w["na_k_g"][l], NA_HEADS)),
        "gw": gw.astype(BF16),
        "gb": row(jnp.concatenate([w["gla_gb_f"][l], w["gla_gb_b"][l]])),
        "conv_w": w["conv_w"][l].astype(F32),
        "conv_b": row(w["conv_b"][l]),
        "conv_ln_g": row(w["conv_ln_g"][l]),
        "conv_ln_b": row(w["conv_ln_b"][l]),
        "conv_pw_w": w["conv_pw_w"][l].astype(BF16),
        "conv_pw_b": row(w["conv_pw_b"][l]),
        "gg": row(jnp.tile(w["gla_out_g"][l], GLA_HEADS)),
        "ones_gla": _block_ones(GLA_V, GLA_DV),
        "w_out": w["w_out"][l].astype(BF16),
        "g2": row(w["norm2_g"][l]),
        "w1": w["w_mlp_in"][l].astype(BF16),
        "w2": w["w_mlp_out"][l].astype(BF16),
    }


def kernel(x, c, ctx, c_ctx, w_ada, b_ada, norm1_g, w_in, na_q_g, na_k_g, na_rpb, conv_w, conv_b, conv_ln_g,
           conv_ln_b, conv_pw_w, conv_pw_b, gla_gw_f, gla_gb_f, gla_gw_b, gla_gb_b, gla_out_g, w_out, norm2_g,
           w_mlp_in, w_mlp_out):
    w = dict(norm1_g=norm1_g, w_in=w_in, na_q_g=na_q_g, na_k_g=na_k_g, conv_w=conv_w, conv_b=conv_b,
             conv_ln_g=conv_ln_g, conv_ln_b=conv_ln_b, conv_pw_w=conv_pw_w, conv_pw_b=conv_pw_b,
             gla_gw_f=gla_gw_f, gla_gb_f=gla_gb_f, gla_gw_b=gla_gw_b, gla_gb_b=gla_gb_b, gla_out_g=gla_out_g,
             w_out=w_out, norm2_g=norm2_g, w_mlp_in=w_mlp_in, w_mlp_out=w_mlp_out)
    b, n, d = x.shape
    lc = ctx.shape[1]
    depth = w_ada.shape[0]
    n_rows = n // GRID_W
    assert n % GRID_W == 0 and n_rows % NA_TILE_ROWS == 0 and n_rows >= NA_KEY_ROWS + 1
    tm_x = tm_c = tn_c = 256
    tn_x = 512 if n % 512 == 0 else 256
    tm_in = 512 if n % 512 == 0 else tm_x
    assert n % tm_x == 0 and lc % tm_c == 0

    ada_rows = -(-(b + 1) // 8) * 8
    cc = jnp.concatenate([c, c_ctx[None, :], jnp.zeros((ada_rows - b - 1, d), c.dtype)], axis=0).astype(F32)
    mod_all = _ada(cc, w_ada, b_ada)

    rope_tabs = _rope_tables(n)
    bias_blocks = _na_bias_blocks(na_rpb)
    gla_consts = _gla_constants()
    zero_state = jnp.zeros((b, GLA_V, GLA_K), F32)

    xc = ctx
    for l in range(depth):
        lw = _layer_weights(l, w)
        mod = mod_all[l].reshape(ada_rows, 6, d)
        mod_x = mod[:b]
        mod_c = mod[b:b + 1]
        need_ctx_out = l < depth - 1

        (qac, kac, vac, hcc, qgc, kgc, vgc, src, cfc, tfc, cbc, tbc) = _inproj(xc, mod_c, lw, None, tm_c)
        (qa, ka, va, hc, qg, kg, vg, sr, cf, tf, cb, tb) = _inproj(x, mod_x, lw, rope_tabs, tm_in)

        ofc, obc, s_f, s_b = _gla(qgc, kgc, vgc, cfc, tfc, cbc, tbc, zero_state, zero_state, gla_consts, tn_c)
        of, obw, _, _ = _gla(qg, kg, vg, cf, tf, cb, tb, s_f, s_b, gla_consts, tn_x)

        oa = _na_attn(qa, ka, va, kac, vac, bias_blocks, l, n_rows)
        x = _outmlp(x, oa, hc, of, obw, sr, mod_x, lw, tm_in)

        if need_ctx_out:
            oac = _ctx_attn(qac, kac, vac)
            xc = _outmlp(xc, oac, hcc, ofc, obc, src, mod_c, lw, tm_c)
    return x
```
